```python
import math
import jax
import jax.numpy as jnp
from jax import lax
import numpy as np

D_MODEL = 1024
BATCH = 8
SEQ = 4096
DEPTH = 4

CTX_LEN = 256
GRID_W = 64
N_DIR = 2
EPS = 1e-6

GLA_HEADS = 4
GLA_VAL = D_MODEL // 2
GLA_KEY = GLA_VAL // 2
GLA_DV = GLA_VAL // GLA_HEADS
GLA_DK = GLA_KEY // GLA_HEADS
GLA_RANK = 16
GLA_TAU = 16.0
GLA_CHUNK = 64

LRU_WIDTH = D_MODEL // 4
LRU_BLOCKS = 4
LRU_BLOCK = LRU_WIDTH // LRU_BLOCKS
LRU_CONV = 4
LRU_C = 8.0

S5_WIDTH = D_MODEL // 4
S5_GROUP = 16
S5_GROUPS = S5_WIDTH // S5_GROUP
S5_STATE = 64

MIX_WIDTH = GLA_VAL + LRU_WIDTH + S5_WIDTH
D_FF = 4 * D_MODEL
IN_COLS = 2 * GLA_KEY + 2 * GLA_VAL + N_DIR * GLA_RANK + 2 * LRU_WIDTH + S5_WIDTH

kernel_name = "hybrid_gla_rglru_s5_prefix_dit"


def rmsnorm(x, g):
    xf = x.astype(jnp.float32)
    y = xf * lax.rsqrt(jnp.mean(xf * xf, axis=-1, keepdims=True) + EPS)
    return (y * g.astype(jnp.float32)).astype(x.dtype)


def flip_seq(t):
    return jnp.flip(t, axis=1)


def to_col_major(t, rows):
    b, l, ch = t.shape
    return t.reshape(b, rows, GRID_W, ch).transpose(0, 2, 1, 3).reshape(b, l, ch)


def from_col_major(t, rows):
    b, l, ch = t.shape
    return t.reshape(b, GRID_W, rows, ch).transpose(0, 2, 1, 3).reshape(b, l, ch)


def split_projection(z):
    sizes = (GLA_KEY, GLA_KEY, GLA_VAL, GLA_VAL, N_DIR * GLA_RANK, LRU_WIDTH, LRU_WIDTH, S5_WIDTH)
    idx = [int(i) for i in np.cumsum(sizes)[:-1]]
    return jnp.split(z, idx, axis=-1)


def linear_scan(a, b):
    def comb(l, r):
        return l[0] * r[0], r[0] * l[1] + r[1]
    return lax.associative_scan(comb, (a, b), axis=1)[1]


def complex_linear_scan(ar, ai, br, bi):
    def comb(l, r):
        ar1, ai1, br1, bi1 = l
        ar2, ai2, br2, bi2 = r
        return (ar2 * ar1 - ai2 * ai1, ar2 * ai1 + ai2 * ar1,
                ar2 * br1 - ai2 * bi1 + br2, ar2 * bi1 + ai2 * br1 + bi2)
    _, _, hr, hi = lax.associative_scan(comb, (ar, ai, br, bi), axis=1)
    return hr, hi


def bidirectional(run, ctx_in, lat_in):
    y_ctx, y_lat = None, None
    for d in range(N_DIR):
        ci, li = ctx_in, lat_in
        if d == 1:
            ci, li = tuple(map(flip_seq, ci)), tuple(map(flip_seq, li))
        yc, h_ctx = run(d, ci, None)
        yl, _ = run(d, li, h_ctx)
        if d == 1:
            yc, yl = flip_seq(yc), flip_seq(yl)
        y_ctx = yc if y_ctx is None else y_ctx + yc
        y_lat = yl if y_lat is None else y_lat + yl
    return y_ctx, y_lat


def gla_chunked(q, k, v, log_a, s0):
    bsz, L, H, _ = q.shape
    DV = v.shape[-1]
    C = GLA_CHUNK
    n = L // C

    def chunk(t):
        return t.astype(jnp.float32).reshape(bsz, n, C, H, t.shape[-1])

    q, k, v, log_a = chunk(q), chunk(k), chunk(v), chunk(log_a)
    b = jnp.cumsum(log_a, axis=2)
    b_last = b[:, :, -1]
    q_dec = q * jnp.exp(b)
    k_inv = k * jnp.exp(-b)
    k_end = k * jnp.exp(b_last[:, :, None] - b)
    lower = jnp.tril(jnp.ones((C, C), dtype=bool))
    scores = jnp.where(lower, jnp.einsum('bnihd,bnjhd->bnhij', q_dec, k_inv), 0.0)
    o_intra = jnp.einsum('bnhij,bnjhe->bnihe', scores, v)
    ds = jnp.einsum('bnjhd,bnjhe->bnhde', k_end, v)

    def step(s, inp):
        decay, inc = inp
        return decay[..., None] * s + inc, s

    s_fin, s_in = lax.scan(step, s0, (jnp.moveaxis(jnp.exp(b_last), 1, 0), jnp.moveaxis(ds, 1, 0)))
    o_inter = jnp.einsum('bnihd,nbhde->bnihe', q_dec, s_in)
    return (o_intra + o_inter).reshape(bsz, L, H, DV), s_fin


def gla_mixer(ctx_in, lat_in, up_w, up_b):
    def run(d, inp, h0):
        q, k, v, lr = inp
        bsz, L, _ = q.shape
        logit = (lr[..., d * GLA_RANK:(d + 1) * GLA_RANK].astype(jnp.float32) @ up_w[d].astype(jnp.float32)
                 + up_b[d].astype(jnp.float32))
        log_a = jax.nn.log_sigmoid(logit) / GLA_TAU
        s0 = jnp.zeros((bsz, GLA_HEADS, GLA_DK, GLA_DV), jnp.float32) if h0 is None else h0

        def heads(t, e):
            return t.reshape(bsz, L, GLA_HEADS, e)

        return gla_chunked(heads(q, GLA_DK) * GLA_DK ** -0.5, heads(k, GLA_DK), heads(v, GLA_DV),
                           heads(log_a, GLA_DK), s0)
    return bidirectional(run, ctx_in, lat_in)


def short_conv(x, w, b):
    K, ch = w.shape
    y = lax.conv_general_dilated(x, w[:, None, :].astype(x.dtype), window_strides=(1,),
                                 padding=[(K - 1, 0)], dimension_numbers=('NWC', 'WIO', 'NWC'),
                                 feature_group_count=ch)
    return y + b.astype(x.dtype)


def rglru_mixer(ctx_in, lat_in, conv_w, conv_b, wa, ba, wx, bx, lam):
    def run(d, inp, h0):
        (xb,) = inp
        xc = short_conv(xb, conv_w[d], conv_b[d]).astype(jnp.float32)
        bsz, L, _ = xc.shape
        xr = xc.reshape(bsz, L, LRU_BLOCKS, LRU_BLOCK)
        r = jax.nn.sigmoid(jnp.einsum('blnj,njk->blnk', xr, wa[d].astype(jnp.float32)).reshape(bsz, L, LRU_WIDTH)
                           + ba[d].astype(jnp.float32))
        i = jax.nn.sigmoid(jnp.einsum('blnj,njk->blnk', xr, wx[d].astype(jnp.float32)).reshape(bsz, L, LRU_WIDTH)
                           + bx[d].astype(jnp.float32))
        log_a = -LRU_C * r * jax.nn.softplus(-lam[d].astype(jnp.float32))
        a = jnp.exp(log_a)
        bt = jnp.sqrt(-jnp.expm1(2.0 * log_a)) * (i * xc)
        if h0 is not None:
            bt = bt.at[:, 0].add(a[:, 0] * h0)
        h = linear_scan(a, bt)
        return h, h[:, -1]
    return bidirectional(run, ctx_in, lat_in)


def s5_mixer(ctx_in, lat_in, lam_re, lam_im, log_dt, b_re, b_im, c_re, c_im):
    def run(d, inp, h0):
        (u,) = inp
        bsz, L, _ = u.shape
        lr_ = lam_re[d].astype(jnp.float32)
        li_ = lam_im[d].astype(jnp.float32)
        dt = jnp.exp(log_dt[d].astype(jnp.float32))[:, None]
        mag = jnp.exp(lr_ * dt)
        ang = li_ * dt
        abar_r, abar_i = mag * jnp.cos(ang), mag * jnp.sin(ang)
        den = lr_ * lr_ + li_ * li_
        num_r = abar_r - 1.0
        coef_r = (num_r * lr_ + abar_i * li_) / den
        coef_i = (abar_i * lr_ - num_r * li_) / den
        br, bi = b_re[d].astype(jnp.float32), b_im[d].astype(jnp.float32)
        bbar_r = coef_r[..., None] * br - coef_i[..., None] * bi
        bbar_i = coef_r[..., None] * bi + coef_i[..., None] * br
        ug = u.astype(jnp.float32).reshape(bsz, L, S5_GROUPS, S5_GROUP)
        bu_r = jnp.einsum('blgh,gph->blgp', ug, bbar_r)
        bu_i = jnp.einsum('blgh,gph->blgp', ug, bbar_i)
        if h0 is not None:
            hr, hi = h0
            bu_r = bu_r.at[:, 0].add(abar_r * hr - abar_i * hi)
            bu_i = bu_i.at[:, 0].add(abar_r * hi + abar_i * hr)
        shape = bu_r.shape
        xr, xi = complex_linear_scan(jnp.broadcast_to(abar_r, shape), jnp.broadcast_to(abar_i, shape), bu_r, bu_i)
        y = (jnp.einsum('ghp,blgp->blgh', c_re[d].astype(jnp.float32), xr)
             - jnp.einsum('ghp,blgp->blgh', c_im[d].astype(jnp.float32), xi))
        return y.reshape(bsz, L, S5_WIDTH), (xr[:, -1], xi[:, -1])
    return bidirectional(run, ctx_in, lat_in)


def merge_groups(gla_o, gla_gate, lru_h, lru_gate, s5_y, s5_u, gla_norm, s5_d, glu_w, glu_b, w_out, dtype):
    bsz, L = gla_o.shape[:2]
    o = gla_o * lax.rsqrt(jnp.mean(gla_o * gla_o, axis=-1, keepdims=True) + EPS)
    o = o.reshape(bsz, L, GLA_VAL) * gla_norm.astype(jnp.float32) * jax.nn.silu(gla_gate.astype(jnp.float32))
    r = lru_h * jax.nn.gelu(lru_gate.astype(jnp.float32))
    s = jax.nn.gelu(s5_y + s5_d.astype(jnp.float32) * s5_u.astype(jnp.float32))
    s = s * jax.nn.sigmoid(s @ glu_w.astype(jnp.float32) + glu_b.astype(jnp.float32))
    cat = jnp.concatenate([o, r, s], axis=-1).astype(dtype)
    return cat @ w_out


def sqrelu_mlp(h, w1, w2):
    return jnp.square(jax.nn.relu(h @ w1)) @ w2


def setup_inputs(seed: int = 0) -> dict:
    key = jax.random.key(seed)
    ks = iter(jax.random.split(key, 48))
    Ld, D = DEPTH, D_MODEL

    def nrm(shape, scale):
        return jax.random.normal(next(ks), shape, jnp.float32) * scale

    x = nrm((BATCH, SEQ, D), 1.0)
    c = nrm((BATCH, D), 1.0)
    ctx = nrm((BATCH, CTX_LEN, D), 1.0)
    c_ctx = nrm((D,), 1.0)
    w_mod = nrm((Ld, D, 6 * D), 0.5 * D ** -0.5)
    b_mod = nrm((Ld, 6 * D), 0.02)
    norm1 = 1.0 + nrm((Ld, D), 0.02)
    norm2 = 1.0 + nrm((Ld, D), 0.02)
    w_in = nrm((Ld, D, IN_COLS), D ** -0.5)
    gla_up_w = nrm((Ld, N_DIR, GLA_RANK, GLA_KEY), GLA_RANK ** -0.5)
    gla_up_b = 2.0 + nrm((Ld, N_DIR, GLA_KEY), 0.1)
    gla_norm = 1.0 + nrm((Ld, GLA_VAL), 0.02)
    lru_conv_w = nrm((Ld, N_DIR, LRU_CONV, LRU_WIDTH), LRU_CONV ** -0.5)
    lru_conv_b = nrm((Ld, N_DIR, LRU_WIDTH), 0.01)
    lru_wa = nrm((Ld, N_DIR, LRU_BLOCKS, LRU_BLOCK, LRU_BLOCK), LRU_BLOCK ** -0.5)
    lru_ba = nrm((Ld, N_DIR, LRU_WIDTH), 0.01)
    lru_wx = nrm((Ld, N_DIR, LRU_BLOCKS, LRU_BLOCK, LRU_BLOCK), LRU_BLOCK ** -0.5)
    lru_bx = nrm((Ld, N_DIR, LRU_WIDTH), 0.01)
    a8 = jax.random.uniform(next(ks), (Ld, N_DIR, LRU_WIDTH), jnp.float32, 0.9, 0.999)
    a = jnp.exp(jnp.log(a8) / LRU_C)
    lru_lambda = jnp.log(a) - jnp.log1p(-a)
    n_idx = jnp.arange(S5_STATE, dtype=jnp.float32)
    s5_lam_re = -0.5 * (1.0 + nrm((Ld, N_DIR, S5_GROUPS, S5_STATE), 0.05))
    s5_lam_im = jnp.pi * n_idx + nrm((Ld, N_DIR, S5_GROUPS, S5_STATE), 0.05)
    s5_log_dt = jax.random.uniform(next(ks), (Ld, N_DIR, S5_GROUPS), jnp.float32,
                                   math.log(1e-3), math.log(1e-1))
    s5_b_re = nrm((Ld, N_DIR, S5_GROUPS, S5_STATE, S5_GROUP), (2 * S5_GROUP) ** -0.5)
    s5_b_im = nrm((Ld, N_DIR, S5_GROUPS, S5_STATE, S5_GROUP), (2 * S5_GROUP) ** -0.5)
    s5_c_re = nrm((Ld, N_DIR, S5_GROUPS, S5_GROUP, S5_STATE), 0.5)
    s5_c_im = nrm((Ld, N_DIR, S5_GROUPS, S5_GROUP, S5_STATE), 0.5)
    s5_d = nrm((Ld, S5_WIDTH), 0.5)
    s5_glu_w = nrm((Ld, S5_WIDTH, S5_WIDTH), S5_WIDTH ** -0.5)
    s5_glu_b = nrm((Ld, S5_WIDTH), 0.01)
    w_out = nrm((Ld, MIX_WIDTH, D), MIX_WIDTH ** -0.5)
    w_ff1 = nrm((Ld, D, D_FF), D ** -0.5)
    w_ff2 = nrm((Ld, D_FF, D), D_FF ** -0.5)
    final_norm = 1.0 + nrm((D,), 0.02)
    return {"x": x, "c": c, "ctx": ctx, "c_ctx": c_ctx, "w_mod": w_mod, "b_mod": b_mod,
            "norm1": norm1, "norm2": norm2, "w_in": w_in, "gla_up_w": gla_up_w, "gla_up_b": gla_up_b,
            "gla_norm": gla_norm, "lru_conv_w": lru_conv_w, "lru_conv_b": lru_conv_b, "lru_wa": lru_wa,
            "lru_ba": lru_ba, "lru_wx": lru_wx, "lru_bx": lru_bx, "lru_lambda": lru_lambda,
            "s5_lam_re": s5_lam_re, "s5_lam_im": s5_lam_im, "s5_log_dt": s5_log_dt,
            "s5_b_re": s5_b_re, "s5_b_im": s5_b_im, "s5_c_re": s5_c_re, "s5_c_im": s5_c_im,
            "s5_d": s5_d, "s5_glu_w": s5_glu_w, "s5_glu_b": s5_glu_b, "w_out": w_out,
            "w_ff1": w_ff1, "w_ff2": w_ff2, "final_norm": final_norm}


def reference(x, c, ctx, c_ctx, w_mod, b_mod, norm1, norm2, w_in, gla_up_w, gla_up_b, gla_norm,
              lru_conv_w, lru_conv_b, lru_wa, lru_ba, lru_wx, lru_bx, lru_lambda,
              s5_lam_re, s5_lam_im, s5_log_dt, s5_b_re, s5_b_im, s5_c_re, s5_c_im,
              s5_d, s5_glu_w, s5_glu_b, w_out, w_ff1, w_ff2, final_norm):
    rows = x.shape[1] // GRID_W
    x_lat, x_ctx = x, ctx
    for l in range(DEPTH):
        last = l == DEPTH - 1
        m_lat = (jax.nn.silu(c) @ w_mod[l] + b_mod[l])[:, None, :]
        m_ctx = (jax.nn.silu(c_ctx) @ w_mod[l] + b_mod[l])[None, None, :]
        sh1, sc1, g1, sh2, sc2, g2 = jnp.split(m_lat, 6, axis=-1)
        csh1, csc1, cg1, csh2, csc2, cg2 = jnp.split(m_ctx, 6, axis=-1)

        h_lat = rmsnorm(x_lat, norm1[l]) * (1.0 + sc1) + sh1
        h_ctx = rmsnorm(x_ctx, norm1[l]) * (1.0 + csc1) + csh1
        q_l, k_l, v_l, gg_l, lr_l, lx_l, lg_l, su_l = split_projection(h_lat @ w_in[l])
        q_c, k_c, v_c, gg_c, lr_c, lx_c, lg_c, su_c = split_projection(h_ctx @ w_in[l])

        gla_c, gla_l = gla_mixer((q_c, k_c, v_c, lr_c), (q_l, k_l, v_l, lr_l), gla_up_w[l], gla_up_b[l])
        lru_c, lru_l = rglru_mixer((lx_c,), (lx_l,), lru_conv_w[l], lru_conv_b[l], lru_wa[l], lru_ba[l],
                                   lru_wx[l], lru_bx[l], lru_lambda[l])
        s5_c, s5_l_cm = s5_mixer((su_c,), (to_col_major(su_l, rows),), s5_lam_re[l], s5_lam_im[l],
                                 s5_log_dt[l], s5_b_re[l], s5_b_im[l], s5_c_re[l], s5_c_im[l])
        s5_l = from_col_major(s5_l_cm, rows)

        mix_lat = merge_groups(gla_l, gg_l, lru_l, lg_l, s5_l, su_l, gla_norm[l], s5_d[l],
                               s5_glu_w[l], s5_glu_b[l], w_out[l], x_lat.dtype)
        x_lat = x_lat + g1 * mix_lat
        h2 = rmsnorm(x_lat, norm2[l]) * (1.0 + sc2) + sh2
        x_lat = x_lat + g2 * sqrelu_mlp(h2, w_ff1[l], w_ff2[l])

        if not last:
            mix_ctx = merge_groups(gla_c, gg_c, lru_c, lg_c, s5_c, su_c, gla_norm[l], s5_d[l],
                                   s5_glu_w[l], s5_glu_b[l], w_out[l], x_ctx.dtype)
            x_ctx = x_ctx + cg1 * mix_ctx
            hc2 = rmsnorm(x_ctx, norm2[l]) * (1.0 + csc2) + csh2
            x_ctx = x_ctx + cg2 * sqrelu_mlp(hc2, w_ff1[l], w_ff2[l])
    return rmsnorm(x_lat, final_norm)
```

```python
import functools

import jax
import jax.numpy as jnp
from jax import lax
from jax.experimental import pallas as pl
from jax.experimental.pallas import tpu as pltpu

F32 = jnp.float32
BF16 = jnp.bfloat16

D_MODEL = 1024
GRID_W = 64
N_DIR = 2
EPS = 1e-6

GLA_HEADS = 4
GLA_VAL = 512
GLA_KEY = 256
GLA_DV = 128
GLA_DK = 64
GLA_RANK = 16
GLA_TAU = 16.0
GLA_CHUNK = 64

LRU_WIDTH = 256
LRU_BLOCKS = 4
LRU_BLOCK = 64
LRU_CONV = 4
LRU_C = 8.0

S5_WIDTH = 256
S5_GROUP = 16
S5_GROUPS = 16
S5_STATE = 64
S5_LANES = S5_GROUPS * S5_STATE

D_FF = 4 * D_MODEL
FF_CHUNK = 1024

LR_PAD = 128
COL_QKV = (0, 1024)
COL_GATE = (1024, 1792)
COL_LR = (1792, 1920)
COL_SCAN = (1920, 2432)
IN_PACKED = 2432

VMEM_LIMIT = 56 * 1024 * 1024


def _rmsnorm_rows(x, g):
    ms = jnp.mean(x * x, axis=-1, keepdims=True)
    return x * lax.rsqrt(ms + EPS) * g


def _gelu_tanh(x):
    c = 0.7978845608028654
    return 0.5 * x * (1.0 + jnp.tanh(c * (x + 0.044715 * (x * x * x))))


def _softplus(x):
    return jnp.maximum(x, 0.0) + jnp.log1p(jnp.exp(-jnp.abs(x)))


def _dot(a, b):
    return jnp.dot(a, b, preferred_element_type=F32)


def _mod_kernel(c_ref, w_ref, b_ref, o_ref):
    cs = c_ref[...]
    s = cs * jax.nn.sigmoid(cs)
    o_ref[0] = _dot(s.astype(BF16), w_ref[0].astype(BF16)) + b_ref[0]


def _modulation(cc, w_mod, b_mod):
    depth, d, n = w_mod.shape
    rows = cc.shape[0]
    tn = 1536
    return pl.pallas_call(
        _mod_kernel,
        grid=(depth, n // tn),
        in_specs=[
            pl.BlockSpec((rows, d), lambda l, j: (0, 0)),
            pl.BlockSpec((1, d, tn), lambda l, j: (l, 0, j)),
            pl.BlockSpec((1, 1, tn), lambda l, j: (l, 0, j)),
        ],
        out_specs=pl.BlockSpec((1, rows, tn), lambda l, j: (l, 0, j)),
        out_shape=jax.ShapeDtypeStruct((depth, rows, n), F32),
        compiler_params=pltpu.CompilerParams(
            dimension_semantics=("parallel", "parallel"), vmem_limit_bytes=VMEM_LIMIT),
        name="modulation",
    )(cc, w_mod, b_mod.reshape(depth, 1, n))


def _inproj_kernel(x_ref, m_ref, g_ref, w_ref, qkv_ref, gate_ref, lr_ref, scan_ref):
    h = _rmsnorm_rows(x_ref[0], g_ref[...]) * (1.0 + m_ref[0, 1:2, :]) + m_ref[0, 0:1, :]
    hb = h.astype(BF16)
    for ref, (c0, c1) in ((qkv_ref, COL_QKV), (gate_ref, COL_GATE), (lr_ref, COL_LR), (scan_ref, COL_SCAN)):
        ref[0] = _dot(hb, w_ref[:, c0:c1])


def _inproj(x, mod, mod_per_batch, norm_g, w_packed, tm):
    b, l, d = x.shape
    widths = [c1 - c0 for c0, c1 in (COL_QKV, COL_GATE, COL_LR, COL_SCAN)]
    mod_map = (lambda i, j: (i, 0, 0)) if mod_per_batch else (lambda i, j: (0, 0, 0))
    return pl.pallas_call(
        _inproj_kernel,
        grid=(b, l // tm),
        in_specs=[
            pl.BlockSpec((1, tm, d), lambda i, j: (i, j, 0)),
            pl.BlockSpec((1, 6, d), mod_map),
            pl.BlockSpec((1, d), lambda i, j: (0, 0)),
            pl.BlockSpec((d, IN_PACKED), lambda i, j: (0, 0)),
        ],
        out_specs=[pl.BlockSpec((1, tm, w), lambda i, j: (i, j, 0)) for w in widths],
        out_shape=[jax.ShapeDtypeStruct((b, l, w), F32) for w in widths],
        compiler_params=pltpu.CompilerParams(
            dimension_semantics=("parallel", "parallel"), vmem_limit_bytes=VMEM_LIMIT),
        name="inproj",
    )(x, mod, norm_g, w_packed)


def _gla_kernel(qkv_ref, lr_ref, upw_ref, upb_ref, s0_ref, o_ref, sfin_ref, st_ref, *, rev, tl):
    j = pl.program_id(1)

    @pl.when(j == 0)
    def _():
        st_ref[...] = s0_ref[0]

    c = GLA_CHUNK
    nchunk = tl // c
    row = lax.broadcasted_iota(jnp.int32, (c, c), 0)
    col = lax.broadcasted_iota(jnp.int32, (c, c), 1)
    keep = (col >= row) if rev else (col <= row)
    tri = keep.astype(BF16)
    nt_dims = (((1,), (1,)), ((), ()))
    tn_dims = (((0,), (0,)), ((), ()))

    def chunk(ci, carry):
        cidx = (nchunk - 1 - ci) if rev else ci
        r0 = pl.multiple_of(cidx * c, c)
        rows = pl.ds(r0, c)
        logit = _dot(lr_ref[0, rows, :].astype(BF16), upw_ref[...]) + upb_ref[...]
        log_a = (jnp.minimum(logit, 0.0) - jnp.log1p(jnp.exp(-jnp.abs(logit)))) * (1.0 / GLA_TAU)
        hi = log_a.astype(BF16)
        r1 = log_a - hi.astype(F32)
        mid = r1.astype(BF16)
        lo = (r1 - mid.astype(F32)).astype(BF16)
        bcum = _dot(tri, hi) + _dot(tri, mid) + _dot(tri, lo)
        blast = bcum[0:1, :] if rev else bcum[c - 1:c, :]
        q = qkv_ref[0, rows, 0:GLA_KEY]
        k = qkv_ref[0, rows, GLA_KEY:2 * GLA_KEY]
        v = qkv_ref[0, rows, 2 * GLA_KEY:2 * GLA_KEY + GLA_VAL].astype(BF16)
        q_dec = ((q * (GLA_DK ** -0.5)) * jnp.exp(bcum)).astype(BF16)
        k_inv = (k * jnp.exp(-bcum)).astype(BF16)
        k_end = (k * jnp.exp(blast - bcum)).astype(BF16)
        decay = jnp.exp(blast)
        for h in range(GLA_HEADS):
            ks = slice(h * GLA_DK, (h + 1) * GLA_DK)
            vs = slice(h * GLA_DV, (h + 1) * GLA_DV)
            qh, kh, keh, vh = q_dec[:, ks], k_inv[:, ks], k_end[:, ks], v[:, vs]
            sc = lax.dot_general(qh, kh, nt_dims, preferred_element_type=F32)
            sc = jnp.where(keep, sc, 0.0).astype(BF16)
            st = st_ref[h]
            o = _dot(sc, vh) + lax.dot_general(qh, st.astype(BF16), nt_dims, preferred_element_type=F32)
            o_ref[0, rows, vs] = o
            ds = lax.dot_general(vh, keh, tn_dims, preferred_element_type=F32)
            st_ref[h] = st * decay[:, ks] + ds
        return carry

    lax.fori_loop(0, nchunk, chunk, 0)

    @pl.when(j == pl.num_programs(1) - 1)
    def _():
        sfin_ref[0] = st_ref[...]


def _gla(qkv, lr, upw, upb, s0, rev, tl):
    b, l, _ = qkv.shape
    nt = l // tl
    tmap = (lambda i, j: (i, nt - 1 - j, 0)) if rev else (lambda i, j: (i, j, 0))
    state_shape = (b, GLA_HEADS, GLA_DV, GLA_DK)
    return pl.pallas_call(
        functools.partial(_gla_kernel, rev=rev, tl=tl),
        grid=(b, nt),
        in_specs=[
            pl.BlockSpec((1, tl, 2 * GLA_KEY + GLA_VAL), tmap),
            pl.BlockSpec((1, tl, LR_PAD), tmap),
            pl.BlockSpec((LR_PAD, GLA_KEY), lambda i, j: (0, 0)),
            pl.BlockSpec((1, GLA_KEY), lambda i, j: (0, 0)),
            pl.BlockSpec((1,) + state_shape[1:], lambda i, j: (i, 0, 0, 0)),
        ],
        out_specs=[
            pl.BlockSpec((1, tl, GLA_VAL), tmap),
            pl.BlockSpec((1,) + state_shape[1:], lambda i, j: (i, 0, 0, 0)),
        ],
        out_shape=[jax.ShapeDtypeStruct((b, l, GLA_VAL), F32), jax.ShapeDtypeStruct(state_shape, F32)],
        scratch_shapes=[pltpu.VMEM(state_shape[1:], F32)],
        compiler_params=pltpu.CompilerParams(
            dimension_semantics=("parallel", "arbitrary"), vmem_limit_bytes=VMEM_LIMIT),
        name="gla_rev" if rev else "gla_fwd",
    )(qkv, lr, upw, upb, s0)


def _lru_kernel(x_ref, cw_ref, cb_ref, wa_ref, ba_ref, wx_ref, bx_ref, lam_ref, h0_ref,
                o_ref, hfin_ref, halo_ref, h_ref, a_scr, b_scr, *, rev, tc):
    j = pl.program_id(0)
    nb = x_ref.shape[0]
    taps = LRU_CONV - 1

    @pl.when(j == 0)
    def _():
        halo_ref[...] = jnp.zeros_like(halo_ref)
        h_ref[...] = h0_ref[...]

    xt = pltpu.einshape("btc->tbc", x_ref[...])
    cw = cw_ref[...]
    if rev:
        ext = jnp.concatenate([xt, halo_ref[...]], axis=0)
        xc = sum(cw[k:k + 1, :] * ext[taps - k:taps - k + tc] for k in range(LRU_CONV))
        halo_ref[...] = xt[:taps]
    else:
        ext = jnp.concatenate([halo_ref[...], xt], axis=0)
        xc = sum(cw[k:k + 1, :] * ext[k:k + tc] for k in range(LRU_CONV))
        halo_ref[...] = xt[tc - taps:]
    xc = (xc + cb_ref[...]).reshape(tc * nb, LRU_WIDTH)
    xb = xc.astype(BF16)
    r = jax.nn.sigmoid(_dot(xb, wa_ref[...]) + ba_ref[...])
    i = jax.nn.sigmoid(_dot(xb, wx_ref[...]) + bx_ref[...])
    log_a = (-LRU_C) * r * _softplus(-lam_ref[...])
    a = jnp.exp(log_a)
    one_minus_a2 = -jnp.tanh(log_a) * (a * a + 1.0)
    a_scr[...] = a.reshape(tc, nb, LRU_WIDTH)
    b_scr[...] = (jnp.sqrt(one_minus_a2) * (i * xc)).reshape(tc, nb, LRU_WIDTH)

    def step(s, h):
        t = (tc - 1 - s) if rev else s
        h = a_scr[t] * h + b_scr[t]
        b_scr[t] = h
        return h

    h = lax.fori_loop(0, tc, step, h_ref[...], unroll=8)
    h_ref[...] = h
    o_ref[...] = pltpu.einshape("tbc->btc", b_scr[...])

    @pl.when(j == pl.num_programs(0) - 1)
    def _():
        hfin_ref[...] = h


def _lru(scan, cw, cb, wa, ba, wx, bx, lam, h0, rev, tc):
    b, l, _ = scan.shape
    nt = l // tc
    tmap = (lambda j: (0, nt - 1 - j, 0)) if rev else (lambda j: (0, j, 0))
    const2 = lambda j: (0, 0)
    w = LRU_WIDTH
    return pl.pallas_call(
        functools.partial(_lru_kernel, rev=rev, tc=tc),
        grid=(nt,),
        in_specs=[
            pl.BlockSpec((b, tc, w), tmap),
            pl.BlockSpec((LRU_CONV, w), const2),
            pl.BlockSpec((1, w), const2),
            pl.BlockSpec((w, w), const2),
            pl.BlockSpec((1, w), const2),
            pl.BlockSpec((w, w), const2),
            pl.BlockSpec((1, w), const2),
            pl.BlockSpec((1, w), const2),
            pl.BlockSpec((b, w), const2),
        ],
        out_specs=[pl.BlockSpec((b, tc, w), tmap), pl.BlockSpec((b, w), const2)],
        out_shape=[jax.ShapeDtypeStruct((b, l, w), F32), jax.ShapeDtypeStruct((b, w), F32)],
        scratch_shapes=[
            pltpu.VMEM((LRU_CONV - 1, b, w), F32),
            pltpu.VMEM((b, w), F32),
            pltpu.VMEM((tc, b, w), F32),
            pltpu.VMEM((tc, b, w), F32),
        ],
        compiler_params=pltpu.CompilerParams(
            dimension_semantics=("arbitrary",), vmem_limit_bytes=VMEM_LIMIT),
        name="lru_rev" if rev else "lru_fwd",
    )(scan, cw, cb, wa, ba, wx, bx, lam, h0)


def _s5_kernel(u_ref, bbar_ref, cmat_ref, ar_ref, ai_ref, h0_ref, o_ref, hfin_ref, st_ref, bu_scr, *, rev, tc):
    j = pl.program_id(0)
    nb = u_ref.shape[0]
    n = S5_LANES

    @pl.when(j == 0)
    def _():
        st_ref[...] = h0_ref[...]

    u = pltpu.einshape("btc->tbc", u_ref[...]).reshape(tc * nb, S5_WIDTH).astype(BF16)
    bu_scr[...] = _dot(u, bbar_ref[...]).reshape(tc, nb, 2 * n)
    ar = ar_ref[...]
    ai = ai_ref[...]

    def step(s, carry):
        xr, xi = carry
        t = (tc - 1 - s) if rev else s
        nxr = ar * xr - ai * xi + bu_scr[t, :, 0:n]
        nxi = ar * xi + ai * xr + bu_scr[t, :, n:2 * n]
        bu_scr[t, :, 0:n] = nxr
        bu_scr[t, :, n:2 * n] = nxi
        return nxr, nxi

    xr, xi = lax.fori_loop(0, tc, step, (st_ref[0], st_ref[1]), unroll=2)
    st_ref[0] = xr
    st_ref[1] = xi
    y = _dot(bu_scr[...].reshape(tc * nb, 2 * n).astype(BF16), cmat_ref[...])
    o_ref[...] = pltpu.einshape("tbc->btc", y.reshape(tc, nb, S5_WIDTH))

    @pl.when(j == pl.num_programs(0) - 1)
    def _():
        hfin_ref[0] = xr
        hfin_ref[1] = xi


def _s5(u_arr, u_map, out_struct, out_map, nt, tc, bbar, cmat, ar, ai, h0, rev):
    b = u_arr.shape[0]
    n = S5_LANES
    w = S5_WIDTH
    const2 = lambda j: (0, 0)
    const3 = lambda j: (0, 0, 0)
    return pl.pallas_call(
        functools.partial(_s5_kernel, rev=rev, tc=tc),
        grid=(nt,),
        in_specs=[
            pl.BlockSpec((b, tc, w), u_map),
            pl.BlockSpec((w, 2 * n), const2),
            pl.BlockSpec((2 * n, w), const2),
            pl.BlockSpec((b, n), const2),
            pl.BlockSpec((b, n), const2),
            pl.BlockSpec((2, b, n), const3),
        ],
        out_specs=[pl.BlockSpec((b, tc, w), out_map), pl.BlockSpec((2, b, n), const3)],
        out_shape=[out_struct, jax.ShapeDtypeStruct((2, b, n), F32)],
        scratch_shapes=[pltpu.VMEM((2, b, n), F32), pltpu.VMEM((tc, b, 2 * n), F32)],
        compiler_params=pltpu.CompilerParams(
            dimension_semantics=("arbitrary",), vmem_limit_bytes=VMEM_LIMIT),
        name="s5_rev" if rev else "s5_fwd",
    )(u_arr, bbar, cmat, ar, ai, h0)


def _s5_ctx(scan, bbar, cmat, ar, ai, h0, rev, tc):
    b, l, _ = scan.shape
    nt = l // tc
    u_map = (lambda j: (0, nt - 1 - j, 1)) if rev else (lambda j: (0, j, 1))
    o_map = (lambda j: (0, nt - 1 - j, 0)) if rev else (lambda j: (0, j, 0))
    return _s5(scan, u_map, jax.ShapeDtypeStruct((b, l, S5_WIDTH), F32), o_map, nt, tc,
               bbar, cmat, ar, ai, h0, rev)


def _s5_lat(scan, bbar, cmat, ar, ai, h0, rev):
    b, l, c = scan.shape
    rows = l // GRID_W
    u_view = scan.reshape(b, rows, GRID_W * c)
    per_tok = c // S5_WIDTH
    u_map = ((lambda j: (0, 0, per_tok * (GRID_W - 1 - j) + 1)) if rev
             else (lambda j: (0, 0, per_tok * j + 1)))
    o_map = (lambda j: (0, 0, GRID_W - 1 - j)) if rev else (lambda j: (0, 0, j))
    y, hfin = _s5(u_view, u_map, jax.ShapeDtypeStruct((b, rows, GRID_W * S5_WIDTH), F32), o_map,
                  GRID_W, rows, bbar, cmat, ar, ai, h0, rev)
    return y.reshape(b, l, S5_WIDTH), hfin


def _merge_mlp_kernel(x_ref, m_ref, gf_ref, gb_ref, gate_ref, lf_ref, lb_ref, sf_ref, sb_ref, su_ref,
                      gn_ref, sd_ref, gluw_ref, glub_ref, wout_ref, n2_ref, w1_ref, w2_ref, fn_ref,
                      o_ref, *, final):
    x = x_ref[0]
    go = gf_ref[0] + gb_ref[0]
    heads = []
    for h in range(GLA_HEADS):
        oh = go[:, h * GLA_DV:(h + 1) * GLA_DV]
        heads.append(oh * lax.rsqrt(jnp.mean(oh * oh, axis=-1, keepdims=True) + EPS))
    gg = gate_ref[0, :, 0:GLA_VAL]
    lg = gate_ref[0, :, GLA_VAL:GLA_VAL + LRU_WIDTH]
    o = jnp.concatenate(heads, axis=-1) * gn_ref[...] * (gg * jax.nn.sigmoid(gg))
    r = (lf_ref[0] + lb_ref[0]) * _gelu_tanh(lg)
    s = _gelu_tanh(sf_ref[0] + sb_ref[0] + sd_ref[...] * su_ref[0])
    s = s * jax.nn.sigmoid(_dot(s.astype(BF16), gluw_ref[...]) + glub_ref[...])
    cat = jnp.concatenate([o, r, s], axis=-1).astype(BF16)
    x1 = x + m_ref[0, 2:3, :] * _dot(cat, wout_ref[...])
    h2 = _rmsnorm_rows(x1, n2_ref[...]) * (1.0 + m_ref[0, 4:5, :]) + m_ref[0, 3:4, :]
    hb = h2.astype(BF16)
    acc = jnp.zeros_like(x1)
    for c in range(D_FF // FF_CHUNK):
        cs = slice(c * FF_CHUNK, (c + 1) * FF_CHUNK)
        t = jnp.maximum(_dot(hb, w1_ref[:, cs]), 0.0)
        acc = acc + _dot((t * t).astype(BF16), w2_ref[cs, :])
    x2 = x1 + m_ref[0, 5:6, :] * acc
    if final:
        x2 = _rmsnorm_rows(x2, fn_ref[...])
    o_ref[0] = x2


def _merge_mlp(x, mod, mod_per_batch, gla_f, gla_b, gate, lru_f, lru_b, s5_f, s5_b, scan,
               gn, sd, gluw, glub, wout, n2, w1, w2, fn, tm, final):
    b, l, d = x.shape
    tok = lambda w: pl.BlockSpec((1, tm, w), lambda i, j: (i, j, 0))
    mod_map = (lambda i, j: (i, 0, 0)) if mod_per_batch else (lambda i, j: (0, 0, 0))
    const = lambda shape: pl.BlockSpec(shape, lambda i, j: (0, 0), pipeline_mode=pl.Buffered(1))
    return pl.pallas_call(
        functools.partial(_merge_mlp_kernel, final=final),
        grid=(b, l // tm),
        in_specs=[
            tok(d),
            pl.BlockSpec((1, 6, d), mod_map),
            tok(GLA_VAL), tok(GLA_VAL), tok(GLA_VAL + LRU_WIDTH),
            tok(LRU_WIDTH), tok(LRU_WIDTH), tok(S5_WIDTH), tok(S5_WIDTH),
            pl.BlockSpec((1, tm, S5_WIDTH), lambda i, j: (i, j, 1)),
            const((1, GLA_VAL)), const((1, S5_WIDTH)), const((S5_WIDTH, S5_WIDTH)), const((1, S5_WIDTH)),
            const((d, d)), const((1, d)), const((d, D_FF)), const((D_FF, d)), const((1, d)),
        ],
        out_specs=tok(d),
        out_shape=jax.ShapeDtypeStruct((b, l, d), F32),
        compiler_params=pltpu.CompilerParams(
            dimension_semantics=("parallel", "parallel"), vmem_limit_bytes=VMEM_LIMIT),
        name="merge_mlp_final" if final else "merge_mlp",
    )(x, mod, gla_f, gla_b, gate, lru_f, lru_b, s5_f, s5_b, scan, gn, sd, gluw, glub, wout, n2, w1, w2, fn)


def _block_diag(blocks):
    n, r, c = blocks.shape
    eye = jnp.eye(n, dtype=blocks.dtype)
    return (eye[:, None, :, None] * blocks[:, :, None, :]).reshape(n * r, n * c)


def _pack_w_in(w):
    d = w.shape[0]
    q_k_v_gg = w[:, 0:1536]
    lr = w[:, 1536:1568]
    lx = w[:, 1568:1824]
    lg = w[:, 1824:2080]
    su = w[:, 2080:2336]
    pad = jnp.zeros((d, LR_PAD - lr.shape[1]), w.dtype)
    return jnp.concatenate([q_k_v_gg, lg, lr, pad, lx, su], axis=1).astype(BF16)


def _s5_params(lam_re, lam_im, log_dt, b_re, b_im, c_re, c_im, nb):
    dt = jnp.exp(log_dt)[:, None]
    mag = jnp.exp(lam_re * dt)
    ang = lam_im * dt
    abar_r, abar_i = mag * jnp.cos(ang), mag * jnp.sin(ang)
    den = lam_re * lam_re + lam_im * lam_im
    num_r = abar_r - 1.0
    coef_r = (num_r * lam_re + abar_i * lam_im) / den
    coef_i = (abar_i * lam_re - num_r * lam_im) / den
    bbar_r = coef_r[..., None] * b_re - coef_i[..., None] * b_im
    bbar_i = coef_r[..., None] * b_im + coef_i[..., None] * b_re
    bd_in = lambda m: _block_diag(jnp.swapaxes(m, 1, 2))
    bbar = jnp.concatenate([bd_in(bbar_r), bd_in(bbar_i)], axis=1).astype(BF16)
    bd_out = lambda m: _block_diag(jnp.swapaxes(m, 1, 2))
    cmat = jnp.concatenate([bd_out(c_re), -bd_out(c_im)], axis=0).astype(BF16)
    ar = jnp.broadcast_to(abar_r.reshape(1, S5_LANES), (nb, S5_LANES))
    ai = jnp.broadcast_to(abar_i.reshape(1, S5_LANES), (nb, S5_LANES))
    return bbar, cmat, ar, ai


def _tile(n, pref):
    t = min(n, pref)
    while n % t:
        t //= 2
    return t


def kernel(x, c, ctx, c_ctx, w_mod, b_mod, norm1, norm2, w_in, gla_up_w, gla_up_b, gla_norm, lru_conv_w, lru_conv_b, lru_wa, lru_ba, lru_wx, lru_bx, lru_lambda, s5_lam_re, s5_lam_im, s5_log_dt, s5_b_re, s5_b_im, s5_c_re, s5_c_im, s5_d, s5_glu_w, s5_glu_b, w_out, w_ff1, w_ff2, final_norm):
    bsz, seq, d = x.shape
    ctx_len = ctx.shape[1]
    depth = w_mod.shape[0]
    assert d == D_MODEL and seq % GRID_W == 0 and (seq // GRID_W) % 8 == 0
    assert seq % GLA_CHUNK == 0 and ctx_len % GLA_CHUNK == 0

    cc = jnp.concatenate([c, c_ctx[None, :], jnp.zeros((7, d), F32)], axis=0)
    mod_all = _modulation(cc, w_mod, b_mod).reshape(depth, bsz + 8, 6, d)

    tm_lat, tm_ctx = _tile(seq, 512), _tile(ctx_len, 512)
    tl_lat, tl_ctx = _tile(seq, 512), _tile(ctx_len, 512)
    tc_lat, tc_ctx = _tile(seq, 256), _tile(ctx_len, 256)
    s5_tc_ctx = _tile(ctx_len, 64)
    row = lambda v: v.reshape(1, -1)

    x_lat, x_ctx = x, ctx
    for l in range(depth):
        last = l == depth - 1
        mod_lat, mod_ctx = mod_all[l, :bsz], mod_all[l, bsz:bsz + 1]
        w_packed = _pack_w_in(w_in[l])
        z_lat = _inproj(x_lat, mod_lat, True, row(norm1[l]), w_packed, tm_lat)
        z_ctx = _inproj(x_ctx, mod_ctx, False, row(norm1[l]), w_packed, tm_ctx)
        qkv_l, gate_l, lr_l, scan_l = z_lat
        qkv_c, gate_c, lr_c, scan_c = z_ctx

        mix_l, mix_c = [], []
        for dr in range(N_DIR):
            rev = dr == 1
            upw = jnp.zeros((LR_PAD, GLA_KEY), F32).at[dr * GLA_RANK:(dr + 1) * GLA_RANK].set(gla_up_w[l, dr]).astype(BF16)
            upb = row(gla_up_b[l, dr])
            s0 = jnp.zeros((bsz, GLA_HEADS, GLA_DV, GLA_DK), F32)
            g_c, s_c = _gla(qkv_c, lr_c, upw, upb, s0, rev, tl_ctx)
            g_l, _ = _gla(qkv_l, lr_l, upw, upb, s_c, rev, tl_lat)
            lru_args = (lru_conv_w[l, dr], row(lru_conv_b[l, dr]),
                        _block_diag(lru_wa[l, dr]).astype(BF16), row(lru_ba[l, dr]),
                        _block_diag(lru_wx[l, dr]).astype(BF16), row(lru_bx[l, dr]),
                        row(lru_lambda[l, dr]))
            r_c, h_c = _lru(scan_c, *lru_args, jnp.zeros((bsz, LRU_WIDTH), F32), rev, tc_ctx)
            r_l, _ = _lru(scan_l, *lru_args, h_c, rev, tc_lat)
            s5_args = _s5_params(s5_lam_re[l, dr], s5_lam_im[l, dr], s5_log_dt[l, dr], s5_b_re[l, dr],
                                 s5_b_im[l, dr], s5_c_re[l, dr], s5_c_im[l, dr], bsz)
            y_c, x_c = _s5_ctx(scan_c, *s5_args, jnp.zeros((2, bsz, S5_LANES), F32), rev, s5_tc_ctx)
            y_l, _ = _s5_lat(scan_l, *s5_args, x_c, rev)
            mix_l.append((g_l, r_l, y_l))
            mix_c.append((g_c, r_c, y_c))

        merge_w = (row(gla_norm[l]), row(s5_d[l]), s5_glu_w[l].astype(BF16), row(s5_glu_b[l]),
                   w_out[l].astype(BF16), row(norm2[l]), w_ff1[l].astype(BF16), w_ff2[l].astype(BF16),
                   row(final_norm))
        (gf, lf, sf), (gb, lb, sb) = mix_l
        x_lat = _merge_mlp(x_lat, mod_lat, True, gf, gb, gate_l, lf, lb, sf, sb, scan_l, *merge_w,
                           tm_lat, last)
        if not last:
            (gf, lf, sf), (gb, lb, sb) = mix_c
            x_ctx = _merge_mlp(x_ctx, mod_ctx, False, gf, gb, gate_c, lf, lb, sf, sb, scan_c, *merge_w,
                               tm_ctx, False)
    return x_lat
```

```python
import functools

import jax
import jax.numpy as jnp
from jax import lax
from jax.experimental import pallas as pl
from jax.experimental.pallas import tpu as pltpu

F32 = jnp.float32
BF16 = jnp.bfloat16

D_MODEL = 1024
GRID_W = 64
N_DIR = 2
EPS = 1e-6

GLA_HEADS = 4
GLA_VAL = 512
GLA_KEY = 256
GLA_DV = 128
GLA_DK = 64
GLA_RANK = 16
GLA_TAU = 16.0
GLA_CHUNK = 64
GLA_SUPER = 256

LRU_WIDTH = 256
LRU_BLOCKS = 4
LRU_BLOCK = 64
LRU_CONV = 4
LRU_C = 8.0

S5_WIDTH = 256
S5_GROUP = 16
S5_GROUPS = 16
S5_STATE = 64
S5_LANES = S5_GROUPS * S5_STATE

D_FF = 4 * D_MODEL
FF_CHUNK = 1024

LR_PAD = 128
COL_QKV = (0, 1024)
COL_GATE = (1024, 1792)
COL_LR = (1792, 1920)
COL_SCAN = (1920, 2432)
IN_PACKED = 2432

VMEM_LIMIT = 56 * 1024 * 1024


def _rmsnorm_rows(x, g):
    ms = jnp.mean(x * x, axis=-1, keepdims=True)
    return x * lax.rsqrt(ms + EPS) * g


def _gelu_tanh(x):
    c = 0.7978845608028654
    return 0.5 * x * (1.0 + jnp.tanh(c * (x + 0.044715 * (x * x * x))))


def _softplus(x):
    return jnp.maximum(x, 0.0) + jnp.log1p(jnp.exp(-jnp.abs(x)))


def _dot(a, b):
    return jnp.dot(a, b, preferred_element_type=F32)


def _mod_kernel(c_ref, w_ref, b_ref, o_ref):
    cs = c_ref[...]
    s = cs * jax.nn.sigmoid(cs)
    o_ref[0] = _dot(s.astype(BF16), w_ref[0].astype(BF16)) + b_ref[0]


def _modulation(cc, w_mod, b_mod):
    depth, d, n = w_mod.shape
    rows = cc.shape[0]
    tn = 1536
    return pl.pallas_call(
        _mod_kernel,
        grid=(depth, n // tn),
        in_specs=[
            pl.BlockSpec((rows, d), lambda l, j: (0, 0)),
            pl.BlockSpec((1, d, tn), lambda l, j: (l, 0, j)),
            pl.BlockSpec((1, 1, tn), lambda l, j: (l, 0, j)),
        ],
        out_specs=pl.BlockSpec((1, rows, tn), lambda l, j: (l, 0, j)),
        out_shape=jax.ShapeDtypeStruct((depth, rows, n), F32),
        compiler_params=pltpu.CompilerParams(
            dimension_semantics=("parallel", "parallel"), vmem_limit_bytes=VMEM_LIMIT),
        name="modulation",
    )(cc, w_mod, b_mod.reshape(depth, 1, n))


def _inproj_kernel(x_ref, m_ref, g_ref, w_ref, qkv_ref, gate_ref, lr_ref, scan_ref):
    h = _rmsnorm_rows(x_ref[0], g_ref[...]) * (1.0 + m_ref[0, 1:2, :]) + m_ref[0, 0:1, :]
    hb = h.astype(BF16)
    for ref, (c0, c1) in ((qkv_ref, COL_QKV), (gate_ref, COL_GATE), (lr_ref, COL_LR), (scan_ref, COL_SCAN)):
        ref[0] = _dot(hb, w_ref[:, c0:c1])


def _inproj(x, mod, mod_per_batch, norm_g, w_packed, tm):
    b, l, d = x.shape
    widths = [c1 - c0 for c0, c1 in (COL_QKV, COL_GATE, COL_LR, COL_SCAN)]
    mod_map = (lambda i, j: (i, 0, 0)) if mod_per_batch else (lambda i, j: (0, 0, 0))
    return pl.pallas_call(
        _inproj_kernel,
        grid=(b, l // tm),
        in_specs=[
            pl.BlockSpec((1, tm, d), lambda i, j: (i, j, 0)),
            pl.BlockSpec((1, 6, d), mod_map),
            pl.BlockSpec((1, d), lambda i, j: (0, 0)),
            pl.BlockSpec((d, IN_PACKED), lambda i, j: (0, 0)),
        ],
        out_specs=[pl.BlockSpec((1, tm, w), lambda i, j: (i, j, 0)) for w in widths],
        out_shape=[jax.ShapeDtypeStruct((b, l, w), F32) for w in widths],
        compiler_params=pltpu.CompilerParams(
            dimension_semantics=("parallel", "parallel"), vmem_limit_bytes=VMEM_LIMIT),
        name="inproj",
    )(x, mod, norm_g, w_packed)


def _gla_kernel(qkv_ref, lr_ref, upw_ref, upb_ref, s0_ref, o_ref, sfin_ref, st_ref, *, rev, tl):
    j = pl.program_id(1)

    @pl.when(j == 0)
    def _():
        st_ref[...] = s0_ref[0]

    c = GLA_CHUNK
    sb = GLA_SUPER
    cps = sb // c
    nsb = tl // sb
    row = lax.broadcasted_iota(jnp.int32, (sb, sb), 0)
    col = lax.broadcasted_iota(jnp.int32, (sb, sb), 1)
    same_chunk = (row // c) == (col // c)
    keep = same_chunk & ((col >= row) if rev else (col <= row))
    tri = keep.astype(BF16)
    nt_dims = (((1,), (1,)), ((), ()))
    tn_dims = (((0,), (0,)), ((), ()))
    order = (lambda n: range(n - 1, -1, -1)) if rev else range

    for si in order(nsb):
        rows = slice(si * sb, (si + 1) * sb)
        logit = _dot(lr_ref[0, rows, :].astype(BF16), upw_ref[...]) + upb_ref[...]
        log_a = (jnp.minimum(logit, 0.0) - jnp.log1p(jnp.exp(-jnp.abs(logit)))) * (1.0 / GLA_TAU)
        hi = log_a.astype(BF16)
        lo = (log_a - hi.astype(F32)).astype(BF16)
        bcum = _dot(tri, hi) + _dot(tri, lo)
        blast = [bcum[ci * c:ci * c + 1, :] if rev else bcum[(ci + 1) * c - 1:(ci + 1) * c, :]
                 for ci in range(cps)]
        btot = jnp.concatenate([jnp.broadcast_to(bl, (c, GLA_KEY)) for bl in blast], axis=0)
        q = qkv_ref[0, rows, 0:GLA_KEY]
        k = qkv_ref[0, rows, GLA_KEY:2 * GLA_KEY]
        v = qkv_ref[0, rows, 2 * GLA_KEY:2 * GLA_KEY + GLA_VAL].astype(BF16)
        q_dec = ((q * (GLA_DK ** -0.5)) * jnp.exp(bcum)).astype(BF16)
        k_inv = (k * jnp.exp(-bcum)).astype(BF16)
        k_end = (k * jnp.exp(btot - bcum)).astype(BF16)
        decay = [jnp.exp(bl) for bl in blast]
        for h in range(GLA_HEADS):
            ks = slice(h * GLA_DK, (h + 1) * GLA_DK)
            vs = slice(h * GLA_DV, (h + 1) * GLA_DV)
            qh, kh, keh, vh = q_dec[:, ks], k_inv[:, ks], k_end[:, ks], v[:, vs]
            sc = lax.dot_general(qh, kh, nt_dims, preferred_element_type=F32)
            o_intra = _dot(jnp.where(keep, sc, 0.0).astype(BF16), vh)
            st = st_ref[h]
            for ci in order(cps):
                r = slice(ci * c, (ci + 1) * c)
                o_inter = lax.dot_general(qh[r], st.astype(BF16), nt_dims, preferred_element_type=F32)
                o_ref[0, si * sb + ci * c:si * sb + (ci + 1) * c, vs] = o_intra[r] + o_inter
                ds = lax.dot_general(vh[r], keh[r], tn_dims, preferred_element_type=F32)
                st = st * decay[ci][:, ks] + ds
            st_ref[h] = st

    @pl.when(j == pl.num_programs(1) - 1)
    def _():
        sfin_ref[0] = st_ref[...]


def _gla(qkv, lr, upw, upb, s0, rev, tl):
    b, l, _ = qkv.shape
    nt = l // tl
    tmap = (lambda i, j: (i, nt - 1 - j, 0)) if rev else (lambda i, j: (i, j, 0))
    state_shape = (b, GLA_HEADS, GLA_DV, GLA_DK)
    return pl.pallas_call(
        functools.partial(_gla_kernel, rev=rev, tl=tl),
        grid=(b, nt),
        in_specs=[
            pl.BlockSpec((1, tl, 2 * GLA_KEY + GLA_VAL), tmap),
            pl.BlockSpec((1, tl, LR_PAD), tmap),
            pl.BlockSpec((LR_PAD, GLA_KEY), lambda i, j: (0, 0)),
            pl.BlockSpec((1, GLA_KEY), lambda i, j: (0, 0)),
            pl.BlockSpec((1,) + state_shape[1:], lambda i, j: (i, 0, 0, 0)),
        ],
        out_specs=[
            pl.BlockSpec((1, tl, GLA_VAL), tmap),
            pl.BlockSpec((1,) + state_shape[1:], lambda i, j: (i, 0, 0, 0)),
        ],
        out_shape=[jax.ShapeDtypeStruct((b, l, GLA_VAL), F32), jax.ShapeDtypeStruct(state_shape, F32)],
        scratch_shapes=[pltpu.VMEM(state_shape[1:], F32)],
        compiler_params=pltpu.CompilerParams(
            dimension_semantics=("parallel", "arbitrary"), vmem_limit_bytes=VMEM_LIMIT),
        name="gla_rev" if rev else "gla_fwd",
    )(qkv, lr, upw, upb, s0)


def _lru_kernel(x_ref, cw_ref, cb_ref, wa_ref, ba_ref, wx_ref, bx_ref, lam_ref, h0_ref,
                o_ref, hfin_ref, halo_ref, h_ref, a_scr, b_scr, *, rev, tc):
    j = pl.program_id(0)
    nb = x_ref.shape[0]
    taps = LRU_CONV - 1

    @pl.when(j == 0)
    def _():
        halo_ref[...] = jnp.zeros_like(halo_ref)
        h_ref[...] = h0_ref[...]

    xt = jnp.swapaxes(x_ref[...], 0, 1)
    cw = cw_ref[...]
    if rev:
        ext = jnp.concatenate([xt, halo_ref[...]], axis=0)
        xc = sum(cw[k:k + 1, :] * ext[taps - k:taps - k + tc] for k in range(LRU_CONV))
        halo_ref[...] = xt[:taps]
    else:
        ext = jnp.concatenate([halo_ref[...], xt], axis=0)
        xc = sum(cw[k:k + 1, :] * ext[k:k + tc] for k in range(LRU_CONV))
        halo_ref[...] = xt[tc - taps:]
    xc = (xc + cb_ref[...]).reshape(tc * nb, LRU_WIDTH)
    xb = xc.astype(BF16)
    r = jax.nn.sigmoid(_dot(xb, wa_ref[...]) + ba_ref[...])
    i = jax.nn.sigmoid(_dot(xb, wx_ref[...]) + bx_ref[...])
    log_a = (-LRU_C) * r * _softplus(-lam_ref[...])
    a = jnp.exp(log_a)
    one_minus_a2 = -jnp.tanh(log_a) * (a * a + 1.0)
    a_scr[...] = a.reshape(tc, nb, LRU_WIDTH)
    b_scr[...] = (jnp.sqrt(one_minus_a2) * (i * xc)).reshape(tc, nb, LRU_WIDTH)

    def step(s, h):
        t = (tc - 1 - s) if rev else s
        h = a_scr[t] * h + b_scr[t]
        b_scr[t] = h
        return h

    h = lax.fori_loop(0, tc, step, h_ref[...], unroll=8)
    h_ref[...] = h
    o_ref[...] = jnp.swapaxes(b_scr[...], 0, 1)

    @pl.when(j == pl.num_programs(0) - 1)
    def _():
        hfin_ref[...] = h


def _lru(scan, cw, cb, wa, ba, wx, bx, lam, h0, rev, tc):
    b, l, _ = scan.shape
    nt = l // tc
    tmap = (lambda j: (0, nt - 1 - j, 0)) if rev else (lambda j: (0, j, 0))
    const2 = lambda j: (0, 0)
    w = LRU_WIDTH
    return pl.pallas_call(
        functools.partial(_lru_kernel, rev=rev, tc=tc),
        grid=(nt,),
        in_specs=[
            pl.BlockSpec((b, tc, w), tmap),
            pl.BlockSpec((LRU_CONV, w), const2),
            pl.BlockSpec((1, w), const2),
            pl.BlockSpec((w, w), const2),
            pl.BlockSpec((1, w), const2),
            pl.BlockSpec((w, w), const2),
            pl.BlockSpec((1, w), const2),
            pl.BlockSpec((1, w), const2),
            pl.BlockSpec((b, w), const2),
        ],
        out_specs=[pl.BlockSpec((b, tc, w), tmap), pl.BlockSpec((b, w), const2)],
        out_shape=[jax.ShapeDtypeStruct((b, l, w), F32), jax.ShapeDtypeStruct((b, w), F32)],
        scratch_shapes=[
            pltpu.VMEM((LRU_CONV - 1, b, w), F32),
            pltpu.VMEM((b, w), F32),
            pltpu.VMEM((tc, b, w), F32),
            pltpu.VMEM((tc, b, w), F32),
        ],
        compiler_params=pltpu.CompilerParams(
            dimension_semantics=("arbitrary",), vmem_limit_bytes=VMEM_LIMIT),
        name="lru_rev" if rev else "lru_fwd",
    )(scan, cw, cb, wa, ba, wx, bx, lam, h0)


def _s5_kernel(u_ref, bbar_ref, cmat_ref, ar_ref, ai_ref, h0_ref, o_ref, hfin_ref, st_ref, bu_scr, *, rev, tc):
    j = pl.program_id(0)
    nb = u_ref.shape[0]
    n = S5_LANES

    @pl.when(j == 0)
    def _():
        st_ref[...] = h0_ref[...]

    u = jnp.swapaxes(u_ref[...], 0, 1).reshape(tc * nb, S5_WIDTH).astype(BF16)
    bu_scr[...] = _dot(u, bbar_ref[...]).reshape(tc, nb, 2 * n)
    ar = ar_ref[...]
    ai = ai_ref[...]

    def step(s, carry):
        xr, xi = carry
        t = (tc - 1 - s) if rev else s
        nxr = ar * xr - ai * xi + bu_scr[t, :, 0:n]
        nxi = ar * xi + ai * xr + bu_scr[t, :, n:2 * n]
        bu_scr[t, :, 0:n] = nxr
        bu_scr[t, :, n:2 * n] = nxi
        return nxr, nxi

    xr, xi = lax.fori_loop(0, tc, step, (st_ref[0], st_ref[1]), unroll=2)
    st_ref[0] = xr
    st_ref[1] = xi
    y = _dot(bu_scr[...].reshape(tc * nb, 2 * n).astype(BF16), cmat_ref[...])
    o_ref[...] = jnp.swapaxes(y.reshape(tc, nb, S5_WIDTH), 0, 1)

    @pl.when(j == pl.num_programs(0) - 1)
    def _():
        hfin_ref[0] = xr
        hfin_ref[1] = xi


def _s5(u_arr, u_map, out_struct, out_map, nt, tc, bbar, cmat, ar, ai, h0, rev):
    b = u_arr.shape[0]
    n = S5_LANES
    w = S5_WIDTH
    const2 = lambda j: (0, 0)
    const3 = lambda j: (0, 0, 0)
    return pl.pallas_call(
        functools.partial(_s5_kernel, rev=rev, tc=tc),
        grid=(nt,),
        in_specs=[
            pl.BlockSpec((b, tc, w), u_map),
            pl.BlockSpec((w, 2 * n), const2),
            pl.BlockSpec((2 * n, w), const2),
            pl.BlockSpec((b, n), const2),
            pl.BlockSpec((b, n), const2),
            pl.BlockSpec((2, b, n), const3),
        ],
        out_specs=[pl.BlockSpec((b, tc, w), out_map), pl.BlockSpec((2, b, n), const3)],
        out_shape=[out_struct, jax.ShapeDtypeStruct((2, b, n), F32)],
        scratch_shapes=[pltpu.VMEM((2, b, n), F32), pltpu.VMEM((tc, b, 2 * n), F32)],
        compiler_params=pltpu.CompilerParams(
            dimension_semantics=("arbitrary",), vmem_limit_bytes=VMEM_LIMIT),
        name="s5_rev" if rev else "s5_fwd",
    )(u_arr, bbar, cmat, ar, ai, h0)


def _s5_ctx(scan, bbar, cmat, ar, ai, h0, rev, tc):
    b, l, _ = scan.shape
    nt = l // tc
    u_map = (lambda j: (0, nt - 1 - j, 1)) if rev else (lambda j: (0, j, 1))
    o_map = (lambda j: (0, nt - 1 - j, 0)) if rev else (lambda j: (0, j, 0))
    return _s5(scan, u_map, jax.ShapeDtypeStruct((b, l, S5_WIDTH), F32), o_map, nt, tc,
               bbar, cmat, ar, ai, h0, rev)


def _s5_lat(scan, bbar, cmat, ar, ai, h0, rev):
    b, l, c = scan.shape
    rows = l // GRID_W
    u_view = scan.reshape(b, rows, GRID_W * c)
    per_tok = c // S5_WIDTH
    u_map = ((lambda j: (0, 0, per_tok * (GRID_W - 1 - j) + 1)) if rev
             else (lambda j: (0, 0, per_tok * j + 1)))
    o_map = (lambda j: (0, 0, GRID_W - 1 - j)) if rev else (lambda j: (0, 0, j))
    y, hfin = _s5(u_view, u_map, jax.ShapeDtypeStruct((b, rows, GRID_W * S5_WIDTH), F32), o_map,
                  GRID_W, rows, bbar, cmat, ar, ai, h0, rev)
    return y.reshape(b, l, S5_WIDTH), hfin


def _merge_mlp_kernel(x_ref, m_ref, gf_ref, gb_ref, gate_ref, lf_ref, lb_ref, sf_ref, sb_ref, su_ref,
                      gn_ref, sd_ref, gluw_ref, glub_ref, wout_ref, n2_ref, w1_ref, w2_ref, fn_ref,
                      o_ref, *, final):
    x = x_ref[0]
    go = gf_ref[0] + gb_ref[0]
    heads = []
    for h in range(GLA_HEADS):
        oh = go[:, h * GLA_DV:(h + 1) * GLA_DV]
        heads.append(oh * lax.rsqrt(jnp.mean(oh * oh, axis=-1, keepdims=True) + EPS))
    gg = gate_ref[0, :, 0:GLA_VAL]
    lg = gate_ref[0, :, GLA_VAL:GLA_VAL + LRU_WIDTH]
    o = jnp.concatenate(heads, axis=-1) * gn_ref[...] * (gg * jax.nn.sigmoid(gg))
    r = (lf_ref[0] + lb_ref[0]) * _gelu_tanh(lg)
    s = _gelu_tanh(sf_ref[0] + sb_ref[0] + sd_ref[...] * su_ref[0])
    s = s * jax.nn.sigmoid(_dot(s.astype(BF16), gluw_ref[...]) + glub_ref[...])
    cat = jnp.concatenate([o, r, s], axis=-1).astype(BF16)
    x1 = x + m_ref[0, 2:3, :] * _dot(cat, wout_ref[...])
    h2 = _rmsnorm_rows(x1, n2_ref[...]) * (1.0 + m_ref[0, 4:5, :]) + m_ref[0, 3:4, :]
    hb = h2.astype(BF16)
    acc = jnp.zeros_like(x1)
    for c in range(D_FF // FF_CHUNK):
        cs = slice(c * FF_CHUNK, (c + 1) * FF_CHUNK)
        t = jnp.maximum(_dot(hb, w1_ref[:, cs]), 0.0)
        acc = acc + _dot((t * t).astype(BF16), w2_ref[cs, :])
    x2 = x1 + m_ref[0, 5:6, :] * acc
    if final:
        x2 = _rmsnorm_rows(x2, fn_ref[...])
    o_ref[0] = x2


def _merge_mlp(x, mod, mod_per_batch, gla_f, gla_b, gate, lru_f, lru_b, s5_f, s5_b, scan,
               gn, sd, gluw, glub, wout, n2, w1, w2, fn, tm, final):
    b, l, d = x.shape
    tok = lambda w: pl.BlockSpec((1, tm, w), lambda i, j: (i, j, 0))
    mod_map = (lambda i, j: (i, 0, 0)) if mod_per_batch else (lambda i, j: (0, 0, 0))
    const = lambda shape: pl.BlockSpec(shape, lambda i, j: (0, 0), pipeline_mode=pl.Buffered(1))
    return pl.pallas_call(
        functools.partial(_merge_mlp_kernel, final=final),
        grid=(b, l // tm),
        in_specs=[
            tok(d),
            pl.BlockSpec((1, 6, d), mod_map),
            tok(GLA_VAL), tok(GLA_VAL), tok(GLA_VAL + LRU_WIDTH),
            tok(LRU_WIDTH), tok(LRU_WIDTH), tok(S5_WIDTH), tok(S5_WIDTH),
            pl.BlockSpec((1, tm, S5_WIDTH), lambda i, j: (i, j, 1)),
            const((1, GLA_VAL)), const((1, S5_WIDTH)), const((S5_WIDTH, S5_WIDTH)), const((1, S5_WIDTH)),
            const((d, d)), const((1, d)), const((d, D_FF)), const((D_FF, d)), const((1, d)),
        ],
        out_specs=tok(d),
        out_shape=jax.ShapeDtypeStruct((b, l, d), F32),
        compiler_params=pltpu.CompilerParams(
            dimension_semantics=("parallel", "parallel"), vmem_limit_bytes=VMEM_LIMIT),
        name="merge_mlp_final" if final else "merge_mlp",
    )(x, mod, gla_f, gla_b, gate, lru_f, lru_b, s5_f, s5_b, scan, gn, sd, gluw, glub, wout, n2, w1, w2, fn)


def _block_diag(blocks):
    n, r, c = blocks.shape
    eye = jnp.eye(n, dtype=blocks.dtype)
    return (eye[:, None, :, None] * blocks[:, :, None, :]).reshape(n * r, n * c)


def _pack_w_in(w):
    d = w.shape[0]
    q_k_v_gg = w[:, 0:1536]
    lr = w[:, 1536:1568]
    lx = w[:, 1568:1824]
    lg = w[:, 1824:2080]
    su = w[:, 2080:2336]
    pad = jnp.zeros((d, LR_PAD - lr.shape[1]), w.dtype)
    return jnp.concatenate([q_k_v_gg, lg, lr, pad, lx, su], axis=1).astype(BF16)


def _s5_params(lam_re, lam_im, log_dt, b_re, b_im, c_re, c_im, nb):
    dt = jnp.exp(log_dt)[:, None]
    mag = jnp.exp(lam_re * dt)
    ang = lam_im * dt
    abar_r, abar_i = mag * jnp.cos(ang), mag * jnp.sin(ang)
    den = lam_re * lam_re + lam_im * lam_im
    num_r = abar_r - 1.0
    coef_r = (num_r * lam_re + abar_i * lam_im) / den
    coef_i = (abar_i * lam_re - num_r * lam_im) / den
    bbar_r = coef_r[..., None] * b_re - coef_i[..., None] * b_im
    bbar_i = coef_r[..., None] * b_im + coef_i[..., None] * b_re
    bd_in = lambda m: _block_diag(jnp.swapaxes(m, 1, 2))
    bbar = jnp.concatenate([bd_in(bbar_r), bd_in(bbar_i)], axis=1).astype(BF16)
    bd_out = lambda m: _block_diag(jnp.swapaxes(m, 1, 2))
    cmat = jnp.concatenate([bd_out(c_re), -bd_out(c_im)], axis=0).astype(BF16)
    ar = jnp.broadcast_to(abar_r.reshape(1, S5_LANES), (nb, S5_LANES))
    ai = jnp.broadcast_to(abar_i.reshape(1, S5_LANES), (nb, S5_LANES))
    return bbar, cmat, ar, ai


def _tile(n, pref):
    t = min(n, pref)
    while n % t:
        t //= 2
    return t


def kernel(x, c, ctx, c_ctx, w_mod, b_mod, norm1, norm2, w_in, gla_up_w, gla_up_b, gla_norm, lru_conv_w, lru_conv_b, lru_wa, lru_ba, lru_wx, lru_bx, lru_lambda, s5_lam_re, s5_lam_im, s5_log_dt, s5_b_re, s5_b_im, s5_c_re, s5_c_im, s5_d, s5_glu_w, s5_glu_b, w_out, w_ff1, w_ff2, final_norm):
    bsz, seq, d = x.shape
    ctx_len = ctx.shape[1]
    depth = w_mod.shape[0]
    assert d == D_MODEL and seq % GRID_W == 0 and (seq // GRID_W) % 8 == 0
    assert seq % GLA_SUPER == 0 and ctx_len % GLA_SUPER == 0

    cc = jnp.concatenate([c, c_ctx[None, :], jnp.zeros((7, d), F32)], axis=0)
    mod_all = _modulation(cc, w_mod, b_mod).reshape(depth, bsz + 8, 6, d)

    tm_lat, tm_ctx = _tile(seq, 512), _tile(ctx_len, 512)
    tl_lat, tl_ctx = _tile(seq, 512), _tile(ctx_len, 512)
    tc_lat, tc_ctx = _tile(seq, 256), _tile(ctx_len, 256)
    s5_tc_ctx = _tile(ctx_len, 64)
    row = lambda v: v.reshape(1, -1)

    x_lat, x_ctx = x, ctx
    for l in range(depth):
        last = l == depth - 1
        mod_lat, mod_ctx = mod_all[l, :bsz], mod_all[l, bsz:bsz + 1]
        w_packed = _pack_w_in(w_in[l])
        z_lat = _inproj(x_lat, mod_lat, True, row(norm1[l]), w_packed, tm_lat)
        z_ctx = _inproj(x_ctx, mod_ctx, False, row(norm1[l]), w_packed, tm_ctx)
        qkv_l, gate_l, lr_l, scan_l = z_lat
        qkv_c, gate_c, lr_c, scan_c = z_ctx

        mix_l, mix_c = [], []
        for dr in range(N_DIR):
            rev = dr == 1
            upw = jnp.zeros((LR_PAD, GLA_KEY), F32).at[dr * GLA_RANK:(dr + 1) * GLA_RANK].set(gla_up_w[l, dr]).astype(BF16)
            upb = row(gla_up_b[l, dr])
            s0 = jnp.zeros((bsz, GLA_HEADS, GLA_DV, GLA_DK), F32)
            g_c, s_c = _gla(qkv_c, lr_c, upw, upb, s0, rev, tl_ctx)
            g_l, _ = _gla(qkv_l, lr_l, upw, upb, s_c, rev, tl_lat)
            lru_args = (lru_conv_w[l, dr], row(lru_conv_b[l, dr]),
                        _block_diag(lru_wa[l, dr]).astype(BF16), row(lru_ba[l, dr]),
                        _block_diag(lru_wx[l, dr]).astype(BF16), row(lru_bx[l, dr]),
                        row(lru_lambda[l, dr]))
            r_c, h_c = _lru(scan_c, *lru_args, jnp.zeros((bsz, LRU_WIDTH), F32), rev, tc_ctx)
            r_l, _ = _lru(scan_l, *lru_args, h_c, rev, tc_lat)
            s5_args = _s5_params(s5_lam_re[l, dr], s5_lam_im[l, dr], s5_log_dt[l, dr], s5_b_re[l, dr],
                                 s5_b_im[l, dr], s5_c_re[l, dr], s5_c_im[l, dr], bsz)
            y_c, x_c = _s5_ctx(scan_c, *s5_args, jnp.zeros((2, bsz, S5_LANES), F32), rev, s5_tc_ctx)
            y_l, _ = _s5_lat(scan_l, *s5_args, x_c, rev)
            mix_l.append((g_l, r_l, y_l))
            mix_c.append((g_c, r_c, y_c))

        merge_w = (row(gla_norm[l]), row(s5_d[l]), s5_glu_w[l].astype(BF16), row(s5_glu_b[l]),
                   w_out[l].astype(BF16), row(norm2[l]), w_ff1[l].astype(BF16), w_ff2[l].astype(BF16),
                   row(final_norm))
        (gf, lf, sf), (gb, lb, sb) = mix_l
        x_lat = _merge_mlp(x_lat, mod_lat, True, gf, gb, gate_l, lf, lb, sf, sb, scan_l, *merge_w,
                           tm_lat, last)
        if not last:
            (gf, lf, sf), (gb, lb, sb) = mix_c
            x_ctx = _merge_mlp(x_ctx, mod_ctx, False, gf, gb, gate_c, lf, lb, sf, sb, scan_c, *merge_w,
                               tm_ctx, False)
    return x_lat
```

```python
import functools

import jax
import jax.numpy as jnp
from jax import lax
from jax.experimental import pallas as pl
from jax.experimental.pallas import tpu as pltpu

F32 = jnp.float32
BF16 = jnp.bfloat16

LANES = 128
D_MODEL = 1024
GRID_W = 64
N_DIR = 2
EPS = 1e-6

GLA_HEADS = 4
GLA_VAL = 512
GLA_KEY = 256
GLA_DV = 128
GLA_DK = 64
GLA_RANK = 16
GLA_TAU = 16.0
GLA_CHUNK = 64
GLA_PAIR = 2 * GLA_CHUNK

LRU_WIDTH = 256
LRU_BLOCKS = 4
LRU_BLOCK = 64
LRU_CONV = 4
LRU_C = 8.0

S5_WIDTH = 256
S5_GROUP = 16
S5_GROUPS = 16
S5_STATE = 64
S5_LANES = S5_GROUPS * S5_STATE
S5_MXU_TILE = 256
S5_CHUNK = 64

D_FF = 4 * D_MODEL
FF_CHUNK = 1024

LR_PAD = 128
COL_QKV = (0, 1024)
COL_GATE = (1024, 1792)
COL_LR = (1792, 1920)
COL_LX = (1920, 2176)
COL_SU = (2176, 2432)
IN_PACKED = 2432
GRID_ROWS_PER_TILE = 8
TM_LAT = GRID_ROWS_PER_TILE * GRID_W

VMEM_LIMIT = 56 * 1024 * 1024


def _rmsnorm_rows(x, g):
    ms = jnp.mean(x * x, axis=-1, keepdims=True)
    return x * lax.rsqrt(ms + EPS) * g


def _gelu_tanh(x):
    c = 0.7978845608028654
    return 0.5 * x * (1.0 + jnp.tanh(c * (x + 0.044715 * (x * x * x))))


def _softplus(x):
    return jnp.maximum(x, 0.0) + jnp.log1p(jnp.exp(-jnp.abs(x)))


def _dot(a, b):
    return jnp.dot(a, b, preferred_element_type=F32)


def _mod_kernel(c_ref, w_ref, b_ref, o_ref):
    cs = c_ref[...]
    s = cs * jax.nn.sigmoid(cs)
    o_ref[0] = _dot(s.astype(BF16), w_ref[0].astype(BF16)) + b_ref[0]


def _modulation(cc, w_mod, b_mod):
    depth, d, n = w_mod.shape
    rows = cc.shape[0]
    tn = 1536
    return pl.pallas_call(
        _mod_kernel,
        grid=(depth, n // tn),
        in_specs=[
            pl.BlockSpec((rows, d), lambda l, j: (0, 0)),
            pl.BlockSpec((1, d, tn), lambda l, j: (l, 0, j)),
            pl.BlockSpec((1, 1, tn), lambda l, j: (l, 0, j)),
        ],
        out_specs=pl.BlockSpec((1, rows, tn), lambda l, j: (l, 0, j)),
        out_shape=jax.ShapeDtypeStruct((depth, rows, n), F32),
        compiler_params=pltpu.CompilerParams(
            dimension_semantics=("parallel", "parallel"), vmem_limit_bytes=VMEM_LIMIT),
        name="modulation",
    )(cc, w_mod, b_mod.reshape(depth, 1, n))


def _inproj_kernel(x_ref, m_ref, g_ref, w_ref, qkv_ref, gate_ref, lr_ref, lx_ref, su_ref, *, colmajor):
    h = _rmsnorm_rows(x_ref[0], g_ref[...]) * (1.0 + m_ref[0, 1:2, :]) + m_ref[0, 0:1, :]
    hb = h.astype(BF16)
    for ref, (c0, c1) in ((qkv_ref, COL_QKV), (gate_ref, COL_GATE), (lr_ref, COL_LR), (lx_ref, COL_LX)):
        ref[0] = _dot(hb, w_ref[:, c0:c1])
    su = _dot(hb, w_ref[:, COL_SU[0]:COL_SU[1]])
    if colmajor:
        su_ref[0] = jnp.swapaxes(su.reshape(GRID_ROWS_PER_TILE, GRID_W, S5_WIDTH), 0, 1)
    else:
        su_ref[0] = su


def _inproj(x, mod, mod_per_batch, norm_g, w_packed, tm, colmajor):
    b, l, d = x.shape
    widths = [c1 - c0 for c0, c1 in (COL_QKV, COL_GATE, COL_LR, COL_LX)]
    mod_map = (lambda i, j: (i, 0, 0)) if mod_per_batch else (lambda i, j: (0, 0, 0))
    tok = lambda w: pl.BlockSpec((1, tm, w), lambda i, j: (i, j, 0))
    if colmajor:
        assert tm == TM_LAT
        su_spec = pl.BlockSpec((1, GRID_W, GRID_ROWS_PER_TILE, S5_WIDTH), lambda i, j: (i, 0, j, 0))
        su_shape = jax.ShapeDtypeStruct((b, GRID_W, l // GRID_W, S5_WIDTH), F32)
    else:
        su_spec, su_shape = tok(S5_WIDTH), jax.ShapeDtypeStruct((b, l, S5_WIDTH), F32)
    return pl.pallas_call(
        functools.partial(_inproj_kernel, colmajor=colmajor),
        grid=(b, l // tm),
        in_specs=[
            tok(d),
            pl.BlockSpec((1, 6, d), mod_map),
            pl.BlockSpec((1, d), lambda i, j: (0, 0)),
            pl.BlockSpec((d, IN_PACKED), lambda i, j: (0, 0)),
        ],
        out_specs=[tok(w) for w in widths] + [su_spec],
        out_shape=[jax.ShapeDtypeStruct((b, l, w), F32) for w in widths] + [su_shape],
        compiler_params=pltpu.CompilerParams(
            dimension_semantics=("parallel", "parallel"), vmem_limit_bytes=VMEM_LIMIT),
        name="inproj",
    )(x, mod, norm_g, w_packed)


def _gla_kernel(qkv_ref, lr_ref, upw_ref, upb_ref, s0_ref, o_ref, sfin_ref, st_ref, *, rev, tl):
    j = pl.program_id(1)

    @pl.when(j == 0)
    def _():
        st_ref[...] = s0_ref[0]

    c = GLA_CHUNK
    pr = GLA_PAIR
    nchunk = tl // c
    npair = tl // pr
    row = lax.broadcasted_iota(jnp.int32, (pr, pr), 0)
    col = lax.broadcasted_iota(jnp.int32, (pr, pr), 1)
    keep = ((row // c) == (col // c)) & ((col >= row) if rev else (col <= row))
    tri = keep.astype(BF16)
    nt_dims = (((1,), (1,)), ((), ()))
    tn_dims = (((0,), (0,)), ((), ()))
    order = (lambda n: range(n - 1, -1, -1)) if rev else range
    crow = lambda ci: slice(ci * c, (ci + 1) * c)
    prow = lambda pi: slice(pi * pr, (pi + 1) * pr)
    kcol = lambda h: slice(h * GLA_DK, (h + 1) * GLA_DK)
    vcol = lambda h: slice(h * GLA_DV, (h + 1) * GLA_DV)
    heads = range(GLA_HEADS)

    logit = _dot(lr_ref[0].astype(BF16), upw_ref[...]) + upb_ref[...]
    log_a = (jnp.minimum(logit, 0.0) - jnp.log(1.0 + jnp.exp(-jnp.abs(logit)))) * (1.0 / GLA_TAU)
    hi = log_a.astype(BF16)
    lo = (log_a - hi.astype(F32)).astype(BF16)
    bcum = jnp.concatenate([_dot(tri, hi[prow(pi)]) + _dot(tri, lo[prow(pi)]) for pi in range(npair)], axis=0)
    blast = [bcum[ci * c:ci * c + 1, :] if rev else bcum[(ci + 1) * c - 1:(ci + 1) * c, :]
             for ci in range(nchunk)]
    btot = jnp.concatenate([jnp.broadcast_to(bl, (c, GLA_KEY)) for bl in blast], axis=0)
    q = qkv_ref[0, :, 0:GLA_KEY]
    k = qkv_ref[0, :, GLA_KEY:2 * GLA_KEY]
    v = qkv_ref[0, :, 2 * GLA_KEY:2 * GLA_KEY + GLA_VAL].astype(BF16)
    q_dec = ((q * (GLA_DK ** -0.5)) * jnp.exp(bcum)).astype(BF16)
    k_inv = (k * jnp.exp(-bcum)).astype(BF16)
    k_end = (k * jnp.exp(btot - bcum)).astype(BF16)
    decay = [jnp.exp(bl) for bl in blast]
    sc = {(pi, h): lax.dot_general(q_dec[prow(pi), kcol(h)], k_inv[prow(pi), kcol(h)], nt_dims,
                                   preferred_element_type=F32)
          for pi in range(npair) for h in heads}
    scm = {key: jnp.where(keep, val, 0.0).astype(BF16) for key, val in sc.items()}
    o_intra = {(pi, h): _dot(scm[pi, h], v[prow(pi), vcol(h)]) for pi in range(npair) for h in heads}
    ds = {(ci, h): lax.dot_general(v[crow(ci), vcol(h)], k_end[crow(ci), kcol(h)], tn_dims,
                                   preferred_element_type=F32)
          for ci in range(nchunk) for h in heads}
    s_in = {}
    for h in heads:
        st = st_ref[h]
        for ci in order(nchunk):
            s_in[ci, h] = st.astype(BF16)
            st = st * decay[ci][:, kcol(h)] + ds[ci, h]
        st_ref[h] = st
    o_inter = {(ci, h): lax.dot_general(q_dec[crow(ci), kcol(h)], s_in[ci, h], nt_dims,
                                        preferred_element_type=F32)
               for ci in range(nchunk) for h in heads}
    for ci in range(nchunk):
        half = slice((ci % 2) * c, (ci % 2 + 1) * c)
        o_ref[0, crow(ci), :] = jnp.concatenate(
            [o_intra[ci // 2, h][half] + o_inter[ci, h] for h in heads], axis=-1)

    @pl.when(j == pl.num_programs(1) - 1)
    def _():
        sfin_ref[0] = st_ref[...]


def _gla(qkv, lr, upw, upb, s0, rev, tl):
    b, l, _ = qkv.shape
    nt = l // tl
    tmap = (lambda i, j: (i, nt - 1 - j, 0)) if rev else (lambda i, j: (i, j, 0))
    state_shape = (b, GLA_HEADS, GLA_DV, GLA_DK)
    return pl.pallas_call(
        functools.partial(_gla_kernel, rev=rev, tl=tl),
        grid=(b, nt),
        in_specs=[
            pl.BlockSpec((1, tl, 2 * GLA_KEY + GLA_VAL), tmap),
            pl.BlockSpec((1, tl, LR_PAD), tmap),
            pl.BlockSpec((LR_PAD, GLA_KEY), lambda i, j: (0, 0)),
            pl.BlockSpec((1, GLA_KEY), lambda i, j: (0, 0)),
            pl.BlockSpec((1,) + state_shape[1:], lambda i, j: (i, 0, 0, 0)),
        ],
        out_specs=[
            pl.BlockSpec((1, tl, GLA_VAL), tmap),
            pl.BlockSpec((1,) + state_shape[1:], lambda i, j: (i, 0, 0, 0)),
        ],
        out_shape=[jax.ShapeDtypeStruct((b, l, GLA_VAL), F32), jax.ShapeDtypeStruct(state_shape, F32)],
        scratch_shapes=[pltpu.VMEM(state_shape[1:], F32)],
        compiler_params=pltpu.CompilerParams(
            dimension_semantics=("parallel", "arbitrary"), vmem_limit_bytes=VMEM_LIMIT),
        name="gla_rev" if rev else "gla_fwd",
    )(qkv, lr, upw, upb, s0)


def _to_time_major(src_ref, slab_ref, nb, nt):
    nslab = slab_ref.shape[0]
    for b in range(nb):
        for s in range(nslab):
            slab_ref[s, pl.ds(b, nt, stride=nb), :] = src_ref[b, :, s * LANES:(s + 1) * LANES]
    flat = jnp.concatenate([slab_ref[s] for s in range(nslab)], axis=-1)
    return flat.reshape(nt, nb, nslab * LANES)


def _slabs_to_block(slab_ref, dst_ref, nb, nt):
    for b in range(nb):
        for s in range(slab_ref.shape[0]):
            dst_ref[b, :, s * LANES:(s + 1) * LANES] = slab_ref[s, pl.ds(b, nt, stride=nb), :]


def _lru_kernel(x_ref, cw_ref, cb_ref, wa_ref, ba_ref, wx_ref, bx_ref, lam_ref, h0_ref,
                o_ref, hfin_ref, halo_ref, h_ref, a_scr, b_scr, slab_ref, *, rev, tc):
    j = pl.program_id(0)
    nb = x_ref.shape[0]
    taps = LRU_CONV - 1

    @pl.when(j == 0)
    def _():
        halo_ref[...] = jnp.zeros_like(halo_ref)
        h_ref[...] = h0_ref[...]

    xt = _to_time_major(x_ref, slab_ref, nb, tc)
    cw = cw_ref[...]
    if rev:
        ext = jnp.concatenate([xt, halo_ref[...]], axis=0)
        xc = sum(cw[k:k + 1, :] * ext[taps - k:taps - k + tc] for k in range(LRU_CONV))
        halo_ref[...] = xt[:taps]
    else:
        ext = jnp.concatenate([halo_ref[...], xt], axis=0)
        xc = sum(cw[k:k + 1, :] * ext[k:k + tc] for k in range(LRU_CONV))
        halo_ref[...] = xt[tc - taps:]
    xc = (xc + cb_ref[...]).reshape(tc * nb, LRU_WIDTH)
    xb = xc.astype(BF16)
    r = jax.nn.sigmoid(_dot(xb, wa_ref[...]) + ba_ref[...])
    i = jax.nn.sigmoid(_dot(xb, wx_ref[...]) + bx_ref[...])
    log_a = (-LRU_C) * r * _softplus(-lam_ref[...])
    a = jnp.exp(log_a)
    a_scr[...] = a.reshape(tc, nb, LRU_WIDTH)
    b_scr[...] = (jnp.sqrt(1.0 - a * a) * (i * xc)).reshape(tc, nb, LRU_WIDTH)

    def step(s, h):
        t = (tc - 1 - s) if rev else s
        h = a_scr[t] * h + b_scr[t]
        for sl in range(LRU_WIDTH // LANES):
            slab_ref[sl, pl.ds(pl.multiple_of(t * nb, nb), nb), :] = h[:, sl * LANES:(sl + 1) * LANES]
        return h

    h = lax.fori_loop(0, tc, step, h_ref[...], unroll=8)
    h_ref[...] = h
    _slabs_to_block(slab_ref, o_ref, nb, tc)

    @pl.when(j == pl.num_programs(0) - 1)
    def _():
        hfin_ref[...] = h


def _lru(lx, cw, cb, wa, ba, wx, bx, lam, h0, rev, tc):
    b, l, _ = lx.shape
    nt = l // tc
    tmap = (lambda j: (0, nt - 1 - j, 0)) if rev else (lambda j: (0, j, 0))
    const2 = lambda j: (0, 0)
    w = LRU_WIDTH
    return pl.pallas_call(
        functools.partial(_lru_kernel, rev=rev, tc=tc),
        grid=(nt,),
        in_specs=[
            pl.BlockSpec((b, tc, w), tmap),
            pl.BlockSpec((LRU_CONV, w), const2),
            pl.BlockSpec((1, w), const2),
            pl.BlockSpec((w, w), const2),
            pl.BlockSpec((1, w), const2),
            pl.BlockSpec((w, w), const2),
            pl.BlockSpec((1, w), const2),
            pl.BlockSpec((1, w), const2),
            pl.BlockSpec((b, w), const2),
        ],
        out_specs=[pl.BlockSpec((b, tc, w), tmap), pl.BlockSpec((b, w), const2)],
        out_shape=[jax.ShapeDtypeStruct((b, l, w), F32), jax.ShapeDtypeStruct((b, w), F32)],
        scratch_shapes=[
            pltpu.VMEM((LRU_CONV - 1, b, w), F32),
            pltpu.VMEM((b, w), F32),
            pltpu.VMEM((tc, b, w), F32),
            pltpu.VMEM((tc, b, w), F32),
            pltpu.VMEM((w // LANES, tc * b, LANES), F32),
        ],
        compiler_params=pltpu.CompilerParams(
            dimension_semantics=("arbitrary",), vmem_limit_bytes=VMEM_LIMIT),
        name="lru_rev" if rev else "lru_fwd",
    )(lx, cw, cb, wa, ba, wx, bx, lam, h0)


def _s5_kernel(u_ref, bbar_ref, cmat_ref, ar_ref, ai_ref, h0_ref, o_ref, hfin_ref,
               st_ref, buf_a, buf_b, u_slab, y_slab, *, rev, tc):
    j = pl.program_id(0)
    nb = u_ref.shape[0]
    n = S5_LANES
    tw = S5_MXU_TILE
    ntile = 2 * n // tw
    spt = tc // ntile

    @pl.when(j == 0)
    def _():
        st_ref[...] = h0_ref[...]

    u = _to_time_major(u_ref, u_slab, nb, 2 * tc).reshape(2 * tc * nb, S5_WIDTH).astype(BF16)
    first, second = (1, 0) if rev else (0, 1)
    half_rows = lambda hf: slice(hf * tc * nb, (hf + 1) * tc * nb)
    ar = ar_ref[...]
    ai = ai_ref[...]

    def proj_in(buf, uh, g):
        cols = slice(g * tw, (g + 1) * tw)
        buf[:, :, cols] = _dot(uh, bbar_ref[:, cols]).reshape(tc, nb, tw)

    def scan_steps(buf, carry, g):
        xr, xi = carry
        for s in range(g * spt, (g + 1) * spt):
            t = (tc - 1 - s) if rev else s
            nxr = ar * xr - ai * xi + buf[t, :, 0:n]
            nxi = ar * xi + ai * xr + buf[t, :, n:2 * n]
            buf[t, :, 0:n] = nxr
            buf[t, :, n:2 * n] = nxi
            xr, xi = nxr, nxi
        return xr, xi

    def proj_out(buf, acc, g):
        cols = slice(g * tw, (g + 1) * tw)
        return acc + _dot(buf[:, :, cols].reshape(tc * nb, tw).astype(BF16), cmat_ref[cols, :])

    ua, ub = u[half_rows(first)], u[half_rows(second)]
    for g in range(ntile):
        proj_in(buf_a, ua, g)
    carry = (st_ref[0], st_ref[1])
    for g in range(ntile):
        carry = scan_steps(buf_a, carry, g)
        proj_in(buf_b, ub, g)
    ya = jnp.zeros((tc * nb, S5_WIDTH), F32)
    for g in range(ntile):
        carry = scan_steps(buf_b, carry, g)
        ya = proj_out(buf_a, ya, g)
    yb = jnp.zeros((tc * nb, S5_WIDTH), F32)
    for g in range(ntile):
        yb = proj_out(buf_b, yb, g)
    st_ref[0], st_ref[1] = carry

    for s in range(S5_WIDTH // LANES):
        y_slab[s, half_rows(first), :] = ya[:, s * LANES:(s + 1) * LANES]
        y_slab[s, half_rows(second), :] = yb[:, s * LANES:(s + 1) * LANES]
    _slabs_to_block(y_slab, o_ref, nb, 2 * tc)

    @pl.when(j == pl.num_programs(0) - 1)
    def _():
        hfin_ref[...] = st_ref[...]


def _s5(u_seq, bbar, cmat, ar, ai, h0, rev, tc):
    b, t, w = u_seq.shape
    n = S5_LANES
    nt = t // (2 * tc)
    tmap = (lambda j: (0, nt - 1 - j, 0)) if rev else (lambda j: (0, j, 0))
    const2 = lambda j: (0, 0)
    const3 = lambda j: (0, 0, 0)
    return pl.pallas_call(
        functools.partial(_s5_kernel, rev=rev, tc=tc),
        grid=(nt,),
        in_specs=[
            pl.BlockSpec((b, 2 * tc, w), tmap),
            pl.BlockSpec((w, 2 * n), const2),
            pl.BlockSpec((2 * n, w), const2),
            pl.BlockSpec((b, n), const2),
            pl.BlockSpec((b, n), const2),
            pl.BlockSpec((2, b, n), const3),
        ],
        out_specs=[pl.BlockSpec((b, 2 * tc, w), tmap), pl.BlockSpec((2, b, n), const3)],
        out_shape=[jax.ShapeDtypeStruct((b, t, w), F32), jax.ShapeDtypeStruct((2, b, n), F32)],
        scratch_shapes=[
            pltpu.VMEM((2, b, n), F32),
            pltpu.VMEM((tc, b, 2 * n), F32),
            pltpu.VMEM((tc, b, 2 * n), F32),
            pltpu.VMEM((w // LANES, 2 * tc * b, LANES), F32),
            pltpu.VMEM((w // LANES, 2 * tc * b, LANES), F32),
        ],
        compiler_params=pltpu.CompilerParams(
            dimension_semantics=("arbitrary",), vmem_limit_bytes=VMEM_LIMIT),
        name="s5_rev" if rev else "s5_fwd",
    )(u_seq, bbar, cmat, ar, ai, h0)


def _merge_mlp_kernel(x_ref, m_ref, gf_ref, gb_ref, gate_ref, lf_ref, lb_ref, sf_ref, sb_ref, su_ref,
                      gn_ref, sd_ref, gluw_ref, glub_ref, wout_ref, n2_ref, w1_ref, w2_ref, fn_ref,
                      o_ref, *, final, colmajor):
    x = x_ref[0]
    go = gf_ref[0] + gb_ref[0]
    heads = []
    for h in range(GLA_HEADS):
        oh = go[:, h * GLA_DV:(h + 1) * GLA_DV]
        heads.append(oh * lax.rsqrt(jnp.mean(oh * oh, axis=-1, keepdims=True) + EPS))
    gg = gate_ref[0, :, 0:GLA_VAL]
    lg = gate_ref[0, :, GLA_VAL:GLA_VAL + LRU_WIDTH]
    o = jnp.concatenate(heads, axis=-1) * gn_ref[...] * (gg * jax.nn.sigmoid(gg))
    r = (lf_ref[0] + lb_ref[0]) * _gelu_tanh(lg)
    s = sf_ref[0] + sb_ref[0] + sd_ref[...] * su_ref[0]
    if colmajor:
        s = jnp.swapaxes(s, 0, 1).reshape(TM_LAT, S5_WIDTH)
    s = _gelu_tanh(s)
    s = s * jax.nn.sigmoid(_dot(s.astype(BF16), gluw_ref[...]) + glub_ref[...])
    cat = jnp.concatenate([o, r, s], axis=-1).astype(BF16)
    x1 = x + m_ref[0, 2:3, :] * _dot(cat, wout_ref[...])
    h2 = _rmsnorm_rows(x1, n2_ref[...]) * (1.0 + m_ref[0, 4:5, :]) + m_ref[0, 3:4, :]
    hb = h2.astype(BF16)
    acc = jnp.zeros_like(x1)
    for c in range(D_FF // FF_CHUNK):
        cs = slice(c * FF_CHUNK, (c + 1) * FF_CHUNK)
        t = jnp.maximum(_dot(hb, w1_ref[:, cs]), 0.0)
        acc = acc + _dot((t * t).astype(BF16), w2_ref[cs, :])
    x2 = x1 + m_ref[0, 5:6, :] * acc
    if final:
        x2 = _rmsnorm_rows(x2, fn_ref[...])
    o_ref[0] = x2


def _merge_mlp(x, mod, mod_per_batch, gla_f, gla_b, gate, lru_f, lru_b, s5_f, s5_b, su,
               gn, sd, gluw, glub, wout, n2, w1, w2, fn, tm, final, colmajor):
    b, l, d = x.shape
    tok = lambda w: pl.BlockSpec((1, tm, w), lambda i, j: (i, j, 0))
    if colmajor:
        assert tm == TM_LAT
        s5_tok = pl.BlockSpec((1, GRID_W, GRID_ROWS_PER_TILE, S5_WIDTH), lambda i, j: (i, 0, j, 0))
    else:
        s5_tok = tok(S5_WIDTH)
    mod_map = (lambda i, j: (i, 0, 0)) if mod_per_batch else (lambda i, j: (0, 0, 0))
    const = lambda shape: pl.BlockSpec(shape, lambda i, j: (0, 0), pipeline_mode=pl.Buffered(1))
    return pl.pallas_call(
        functools.partial(_merge_mlp_kernel, final=final, colmajor=colmajor),
        grid=(b, l // tm),
        in_specs=[
            tok(d),
            pl.BlockSpec((1, 6, d), mod_map),
            tok(GLA_VAL), tok(GLA_VAL), tok(GLA_VAL + LRU_WIDTH),
            tok(LRU_WIDTH), tok(LRU_WIDTH), s5_tok, s5_tok, s5_tok,
            const((1, GLA_VAL)), const((1, S5_WIDTH)), const((S5_WIDTH, S5_WIDTH)), const((1, S5_WIDTH)),
            const((d, d)), const((1, d)), const((d, D_FF)), const((D_FF, d)), const((1, d)),
        ],
        out_specs=tok(d),
        out_shape=jax.ShapeDtypeStruct((b, l, d), F32),
        compiler_params=pltpu.CompilerParams(
            dimension_semantics=("parallel", "parallel"), vmem_limit_bytes=VMEM_LIMIT),
        name="merge_mlp_final" if final else "merge_mlp",
    )(x, mod, gla_f, gla_b, gate, lru_f, lru_b, s5_f, s5_b, su, gn, sd, gluw, glub, wout, n2, w1, w2, fn)


def _block_diag(blocks):
    n, r, c = blocks.shape
    eye = jnp.eye(n, dtype=blocks.dtype)
    return (eye[:, None, :, None] * blocks[:, :, None, :]).reshape(n * r, n * c)


def _pack_w_in(w):
    d = w.shape[0]
    q_k_v_gg = w[:, 0:1536]
    lr = w[:, 1536:1568]
    lx = w[:, 1568:1824]
    lg = w[:, 1824:2080]
    su = w[:, 2080:2336]
    pad = jnp.zeros((d, LR_PAD - lr.shape[1]), w.dtype)
    return jnp.concatenate([q_k_v_gg, lg, lr, pad, lx, su], axis=1).astype(BF16)


def _s5_params(lam_re, lam_im, log_dt, b_re, b_im, c_re, c_im, nb):
    dt = jnp.exp(log_dt)[:, None]
    mag = jnp.exp(lam_re * dt)
    ang = lam_im * dt
    abar_r, abar_i = mag * jnp.cos(ang), mag * jnp.sin(ang)
    den = lam_re * lam_re + lam_im * lam_im
    num_r = abar_r - 1.0
    coef_r = (num_r * lam_re + abar_i * lam_im) / den
    coef_i = (abar_i * lam_re - num_r * lam_im) / den
    bbar_r = coef_r[..., None] * b_re - coef_i[..., None] * b_im
    bbar_i = coef_r[..., None] * b_im + coef_i[..., None] * b_re
    bd_in = lambda m: _block_diag(jnp.swapaxes(m, 1, 2))
    bbar = jnp.concatenate([bd_in(bbar_r), bd_in(bbar_i)], axis=1).astype(BF16)
    bd_out = lambda m: _block_diag(jnp.swapaxes(m, 1, 2))
    cmat = jnp.concatenate([bd_out(c_re), -bd_out(c_im)], axis=0).astype(BF16)
    ar = jnp.broadcast_to(abar_r.reshape(1, S5_LANES), (nb, S5_LANES))
    ai = jnp.broadcast_to(abar_i.reshape(1, S5_LANES), (nb, S5_LANES))
    return bbar, cmat, ar, ai


def _tile(n, pref):
    t = min(n, pref)
    while n % t:
        t //= 2
    return t


def kernel(x, c, ctx, c_ctx, w_mod, b_mod, norm1, norm2, w_in, gla_up_w, gla_up_b, gla_norm, lru_conv_w, lru_conv_b, lru_wa, lru_ba, lru_wx, lru_bx, lru_lambda, s5_lam_re, s5_lam_im, s5_log_dt, s5_b_re, s5_b_im, s5_c_re, s5_c_im, s5_d, s5_glu_w, s5_glu_b, w_out, w_ff1, w_ff2, final_norm):
    bsz, seq, d = x.shape
    ctx_len = ctx.shape[1]
    depth = w_mod.shape[0]
    rows = seq // GRID_W
    assert d == D_MODEL and seq == rows * GRID_W and rows % GRID_ROWS_PER_TILE == 0
    assert seq % (2 * S5_CHUNK) == 0 and ctx_len % (2 * S5_CHUNK) == 0

    cc = jnp.concatenate([c, c_ctx[None, :], jnp.zeros((7, d), F32)], axis=0)
    mod_all = _modulation(cc, w_mod, b_mod).reshape(depth, bsz + 8, 6, d)

    tm_ctx = _tile(ctx_len, 512)
    tl_lat, tl_ctx = _tile(seq, 1024), _tile(ctx_len, 1024)
    tc_lat, tc_ctx = _tile(seq, 256), _tile(ctx_len, 256)
    row = lambda v: v.reshape(1, -1)
    cm_view = lambda t: t.reshape(bsz, GRID_W, rows, S5_WIDTH)

    x_lat, x_ctx = x, ctx
    for l in range(depth):
        last = l == depth - 1
        mod_lat, mod_ctx = mod_all[l, :bsz], mod_all[l, bsz:bsz + 1]
        w_packed = _pack_w_in(w_in[l])
        qkv_l, gate_l, lr_l, lx_l, su_l = _inproj(x_lat, mod_lat, True, row(norm1[l]), w_packed, TM_LAT, True)
        qkv_c, gate_c, lr_c, lx_c, su_c = _inproj(x_ctx, mod_ctx, False, row(norm1[l]), w_packed, tm_ctx, False)
        su_seq = su_l.reshape(bsz, seq, S5_WIDTH)

        mix_l, mix_c = [], []
        for dr in range(N_DIR):
            rev = dr == 1
            upw = jnp.zeros((LR_PAD, GLA_KEY), F32).at[dr * GLA_RANK:(dr + 1) * GLA_RANK].set(gla_up_w[l, dr]).astype(BF16)
            upb = row(gla_up_b[l, dr])
            s0 = jnp.zeros((bsz, GLA_HEADS, GLA_DV, GLA_DK), F32)
            g_c, s_c = _gla(qkv_c, lr_c, upw, upb, s0, rev, tl_ctx)
            g_l, _ = _gla(qkv_l, lr_l, upw, upb, s_c, rev, tl_lat)
            lru_args = (lru_conv_w[l, dr], row(lru_conv_b[l, dr]),
                        _block_diag(lru_wa[l, dr]).astype(BF16), row(lru_ba[l, dr]),
                        _block_diag(lru_wx[l, dr]).astype(BF16), row(lru_bx[l, dr]),
                        row(lru_lambda[l, dr]))
            r_c, h_c = _lru(lx_c, *lru_args, jnp.zeros((bsz, LRU_WIDTH), F32), rev, tc_ctx)
            r_l, _ = _lru(lx_l, *lru_args, h_c, rev, tc_lat)
            s5_args = _s5_params(s5_lam_re[l, dr], s5_lam_im[l, dr], s5_log_dt[l, dr], s5_b_re[l, dr],
                                 s5_b_im[l, dr], s5_c_re[l, dr], s5_c_im[l, dr], bsz)
            y_c, x_c = _s5(su_c, *s5_args, jnp.zeros((2, bsz, S5_LANES), F32), rev, S5_CHUNK)
            y_l, _ = _s5(su_seq, *s5_args, x_c, rev, S5_CHUNK)
            mix_l.append((g_l, r_l, cm_view(y_l)))
            mix_c.append((g_c, r_c, y_c))

        merge_w = (row(gla_norm[l]), row(s5_d[l]), s5_glu_w[l].astype(BF16), row(s5_glu_b[l]),
                   w_out[l].astype(BF16), row(norm2[l]), w_ff1[l].astype(BF16), w_ff2[l].astype(BF16),
                   row(final_norm))
        (gf, lf, sf), (gb, lb, sb) = mix_l
        x_lat = _merge_mlp(x_lat, mod_lat, True, gf, gb, gate_l, lf, lb, sf, sb, su_l, *merge_w,
                           TM_LAT, last, True)
        if not last:
            (gf, lf, sf), (gb, lb, sb) = mix_c
            x_ctx = _merge_mlp(x_ctx, mod_ctx, False, gf, gb, gate_c, lf, lb, sf, sb, su_c, *merge_w,
                               tm_ctx, False, False)
    return x_lat
```

```python
import functools

import jax
import jax.numpy as jnp
from jax import lax
from jax.experimental import pallas as pl
from jax.experimental.pallas import tpu as pltpu

F32 = jnp.float32
BF16 = jnp.bfloat16

LANES = 128
D_MODEL = 1024
GRID_W = 64
N_DIR = 2
EPS = 1e-6

GLA_HEADS = 4
GLA_VAL = 512
GLA_KEY = 256
GLA_DV = 128
GLA_DK = 64
GLA_RANK = 16
GLA_TAU = 16.0
GLA_CHUNK = 64
GLA_PAIR = 2 * GLA_CHUNK

LRU_WIDTH = 256
LRU_BLOCKS = 4
LRU_BLOCK = 64
LRU_CONV = 4
LRU_C = 8.0

S5_WIDTH = 256
S5_GROUP = 16
S5_GROUPS = 16
S5_STATE = 64
S5_LANES = S5_GROUPS * S5_STATE
S5_MXU_TILE = 256
S5_CHUNK = 64

D_FF = 4 * D_MODEL
FF_CHUNK = 1024
MERGE_ROW_GROUPS = 2

LR_PAD = 128
COL_QKV = (0, 1024)
COL_GATE = (1024, 1792)
COL_LR = (1792, 1920)
COL_LX = (1920, 2176)
COL_SU = (2176, 2432)
IN_PACKED = 2432
GRID_ROWS_PER_TILE = 8
TM_LAT = GRID_ROWS_PER_TILE * GRID_W

VMEM_LIMIT = 56 * 1024 * 1024


def _rmsnorm_rows(x, g):
    ms = jnp.mean(x * x, axis=-1, keepdims=True)
    return x * lax.rsqrt(ms + EPS) * g


def _gelu_tanh(x):
    c = 0.7978845608028654
    return 0.5 * x * (1.0 + jnp.tanh(c * (x + 0.044715 * (x * x * x))))


def _sigmoid_tanh(x):
    return 0.5 * jnp.tanh(0.5 * x) + 0.5


def _softplus(x):
    return jnp.maximum(x, 0.0) + jnp.log1p(jnp.exp(-jnp.abs(x)))


def _dot(a, b):
    return jnp.dot(a, b, preferred_element_type=F32)


def _mod_kernel(c_ref, w_ref, b_ref, o_ref):
    cs = c_ref[...]
    s = cs * jax.nn.sigmoid(cs)
    o_ref[0] = _dot(s.astype(BF16), w_ref[0].astype(BF16)) + b_ref[0]


def _modulation(cc, w_mod, b_mod):
    depth, d, n = w_mod.shape
    rows = cc.shape[0]
    tn = 1536
    return pl.pallas_call(
        _mod_kernel,
        grid=(depth, n // tn),
        in_specs=[
            pl.BlockSpec((rows, d), lambda l, j: (0, 0)),
            pl.BlockSpec((1, d, tn), lambda l, j: (l, 0, j)),
            pl.BlockSpec((1, 1, tn), lambda l, j: (l, 0, j)),
        ],
        out_specs=pl.BlockSpec((1, rows, tn), lambda l, j: (l, 0, j)),
        out_shape=jax.ShapeDtypeStruct((depth, rows, n), F32),
        compiler_params=pltpu.CompilerParams(
            dimension_semantics=("parallel", "parallel"), vmem_limit_bytes=VMEM_LIMIT),
        name="modulation",
    )(cc, w_mod, b_mod.reshape(depth, 1, n))


def _inproj_kernel(x_ref, m_ref, g_ref, w_ref, qkv_ref, gate_ref, lr_ref, lx_ref, su_ref, *, colmajor):
    h = _rmsnorm_rows(x_ref[0], g_ref[...]) * (1.0 + m_ref[0, 1:2, :]) + m_ref[0, 0:1, :]
    hb = h.astype(BF16)
    for ref, (c0, c1) in ((qkv_ref, COL_QKV), (gate_ref, COL_GATE), (lr_ref, COL_LR), (lx_ref, COL_LX)):
        ref[0] = _dot(hb, w_ref[:, c0:c1])
    su = _dot(hb, w_ref[:, COL_SU[0]:COL_SU[1]])
    if colmajor:
        su_ref[0] = jnp.swapaxes(su.reshape(GRID_ROWS_PER_TILE, GRID_W, S5_WIDTH), 0, 1)
    else:
        su_ref[0] = su


def _inproj(x, mod, mod_per_batch, norm_g, w_packed, tm, colmajor):
    b, l, d = x.shape
    widths = [c1 - c0 for c0, c1 in (COL_QKV, COL_GATE, COL_LR, COL_LX)]
    mod_map = (lambda i, j: (i, 0, 0)) if mod_per_batch else (lambda i, j: (0, 0, 0))
    tok = lambda w: pl.BlockSpec((1, tm, w), lambda i, j: (i, j, 0))
    if colmajor:
        assert tm == TM_LAT
        su_spec = pl.BlockSpec((1, GRID_W, GRID_ROWS_PER_TILE, S5_WIDTH), lambda i, j: (i, 0, j, 0))
        su_shape = jax.ShapeDtypeStruct((b, GRID_W, l // GRID_W, S5_WIDTH), F32)
    else:
        su_spec, su_shape = tok(S5_WIDTH), jax.ShapeDtypeStruct((b, l, S5_WIDTH), F32)
    return pl.pallas_call(
        functools.partial(_inproj_kernel, colmajor=colmajor),
        grid=(b, l // tm),
        in_specs=[
            tok(d),
            pl.BlockSpec((1, 6, d), mod_map),
            pl.BlockSpec((1, d), lambda i, j: (0, 0)),
            pl.BlockSpec((d, IN_PACKED), lambda i, j: (0, 0)),
        ],
        out_specs=[tok(w) for w in widths] + [su_spec],
        out_shape=[jax.ShapeDtypeStruct((b, l, w), F32) for w in widths] + [su_shape],
        compiler_params=pltpu.CompilerParams(
            dimension_semantics=("parallel", "parallel"), vmem_limit_bytes=VMEM_LIMIT),
        name="inproj",
    )(x, mod, norm_g, w_packed)


def _gla_kernel(qkv_ref, lr_ref, upw_ref, upb_ref, s0_ref, o_ref, sfin_ref, st_ref, *, rev, tl):
    j = pl.program_id(1)

    @pl.when(j == 0)
    def _():
        st_ref[...] = s0_ref[0]

    c = GLA_CHUNK
    pr = GLA_PAIR
    nchunk = tl // c
    npair = tl // pr
    row = lax.broadcasted_iota(jnp.int32, (pr, pr), 0)
    col = lax.broadcasted_iota(jnp.int32, (pr, pr), 1)
    keep = ((row // c) == (col // c)) & ((col >= row) if rev else (col <= row))
    tri = keep.astype(BF16)
    nt_dims = (((1,), (1,)), ((), ()))
    tn_dims = (((0,), (0,)), ((), ()))
    order = (lambda n: range(n - 1, -1, -1)) if rev else range
    crow = lambda ci: slice(ci * c, (ci + 1) * c)
    prow = lambda pi: slice(pi * pr, (pi + 1) * pr)
    kcol = lambda h: slice(h * GLA_DK, (h + 1) * GLA_DK)
    vcol = lambda h: slice(h * GLA_DV, (h + 1) * GLA_DV)
    heads = range(GLA_HEADS)

    logit = _dot(lr_ref[0].astype(BF16), upw_ref[...]) + upb_ref[...]
    log_a = (jnp.minimum(logit, 0.0) - jnp.log(1.0 + jnp.exp(-jnp.abs(logit)))) * (1.0 / GLA_TAU)
    hi = log_a.astype(BF16)
    lo = (log_a - hi.astype(F32)).astype(BF16)
    bcum = jnp.concatenate([_dot(tri, hi[prow(pi)]) + _dot(tri, lo[prow(pi)]) for pi in range(npair)], axis=0)
    blast = [bcum[ci * c:ci * c + 1, :] if rev else bcum[(ci + 1) * c - 1:(ci + 1) * c, :]
             for ci in range(nchunk)]
    btot = jnp.concatenate([jnp.broadcast_to(bl, (c, GLA_KEY)) for bl in blast], axis=0)
    q = qkv_ref[0, :, 0:GLA_KEY]
    k = qkv_ref[0, :, GLA_KEY:2 * GLA_KEY]
    v = qkv_ref[0, :, 2 * GLA_KEY:2 * GLA_KEY + GLA_VAL].astype(BF16)
    q_dec = ((q * (GLA_DK ** -0.5)) * jnp.exp(bcum)).astype(BF16)
    k_inv = (k * jnp.exp(-bcum)).astype(BF16)
    k_end = (k * jnp.exp(btot - bcum)).astype(BF16)
    decay = [jnp.exp(bl) for bl in blast]
    sc = {(pi, h): lax.dot_general(q_dec[prow(pi), kcol(h)], k_inv[prow(pi), kcol(h)], nt_dims,
                                   preferred_element_type=F32)
          for pi in range(npair) for h in heads}
    scm = {key: jnp.where(keep, val, 0.0).astype(BF16) for key, val in sc.items()}
    o_intra = {(pi, h): _dot(scm[pi, h], v[prow(pi), vcol(h)]) for pi in range(npair) for h in heads}
    ds = {(ci, h): lax.dot_general(v[crow(ci), vcol(h)], k_end[crow(ci), kcol(h)], tn_dims,
                                   preferred_element_type=F32)
          for ci in range(nchunk) for h in heads}
    s_in = {}
    for h in heads:
        st = st_ref[h]
        for ci in order(nchunk):
            s_in[ci, h] = st.astype(BF16)
            st = st * decay[ci][:, kcol(h)] + ds[ci, h]
        st_ref[h] = st
    o_inter = {(ci, h): lax.dot_general(q_dec[crow(ci), kcol(h)], s_in[ci, h], nt_dims,
                                        preferred_element_type=F32)
               for ci in range(nchunk) for h in heads}
    for ci in range(nchunk):
        half = slice((ci % 2) * c, (ci % 2 + 1) * c)
        o_ref[0, crow(ci), :] = jnp.concatenate(
            [o_intra[ci // 2, h][half] + o_inter[ci, h] for h in heads], axis=-1)

    @pl.when(j == pl.num_programs(1) - 1)
    def _():
        sfin_ref[0] = st_ref[...]


def _gla(qkv, lr, upw, upb, s0, rev, tl):
    b, l, _ = qkv.shape
    nt = l // tl
    tmap = (lambda i, j: (i, nt - 1 - j, 0)) if rev else (lambda i, j: (i, j, 0))
    state_shape = (b, GLA_HEADS, GLA_DV, GLA_DK)
    return pl.pallas_call(
        functools.partial(_gla_kernel, rev=rev, tl=tl),
        grid=(b, nt),
        in_specs=[
            pl.BlockSpec((1, tl, 2 * GLA_KEY + GLA_VAL), tmap),
            pl.BlockSpec((1, tl, LR_PAD), tmap),
            pl.BlockSpec((LR_PAD, GLA_KEY), lambda i, j: (0, 0)),
            pl.BlockSpec((1, GLA_KEY), lambda i, j: (0, 0)),
            pl.BlockSpec((1,) + state_shape[1:], lambda i, j: (i, 0, 0, 0)),
        ],
        out_specs=[
            pl.BlockSpec((1, tl, GLA_VAL), tmap),
            pl.BlockSpec((1,) + state_shape[1:], lambda i, j: (i, 0, 0, 0)),
        ],
        out_shape=[jax.ShapeDtypeStruct((b, l, GLA_VAL), F32), jax.ShapeDtypeStruct(state_shape, F32)],
        scratch_shapes=[pltpu.VMEM(state_shape[1:], F32)],
        compiler_params=pltpu.CompilerParams(
            dimension_semantics=("parallel", "arbitrary"), vmem_limit_bytes=VMEM_LIMIT),
        name="gla_rev" if rev else "gla_fwd",
    )(qkv, lr, upw, upb, s0)


def _to_time_major(src_ref, slab_ref, nb, nt):
    nslab = slab_ref.shape[0]
    for b in range(nb):
        for s in range(nslab):
            slab_ref[s, pl.ds(b, nt, stride=nb), :] = src_ref[b, :, s * LANES:(s + 1) * LANES]
    flat = jnp.concatenate([slab_ref[s] for s in range(nslab)], axis=-1)
    return flat.reshape(nt, nb, nslab * LANES)


def _slabs_to_block(slab_ref, dst_ref, nb, nt):
    for b in range(nb):
        for s in range(slab_ref.shape[0]):
            dst_ref[b, :, s * LANES:(s + 1) * LANES] = slab_ref[s, pl.ds(b, nt, stride=nb), :]


def _lru_kernel(x_ref, cw_ref, cb_ref, wa_ref, ba_ref, wx_ref, bx_ref, lam_ref, h0_ref,
                o_ref, hfin_ref, halo_ref, h_ref, a_scr, b_scr, slab_ref, *, rev, tc):
    j = pl.program_id(0)
    nb = x_ref.shape[0]
    taps = LRU_CONV - 1

    @pl.when(j == 0)
    def _():
        halo_ref[...] = jnp.zeros_like(halo_ref)
        h_ref[...] = h0_ref[...]

    xt = _to_time_major(x_ref, slab_ref, nb, tc)
    cw = cw_ref[...]
    if rev:
        ext = jnp.concatenate([xt, halo_ref[...]], axis=0)
        xc = sum(cw[k:k + 1, :] * ext[taps - k:taps - k + tc] for k in range(LRU_CONV))
        halo_ref[...] = xt[:taps]
    else:
        ext = jnp.concatenate([halo_ref[...], xt], axis=0)
        xc = sum(cw[k:k + 1, :] * ext[k:k + tc] for k in range(LRU_CONV))
        halo_ref[...] = xt[tc - taps:]
    xc = (xc + cb_ref[...]).reshape(tc * nb, LRU_WIDTH)
    xb = xc.astype(BF16)
    r = _sigmoid_tanh(_dot(xb, wa_ref[...]) + ba_ref[...])
    i = _sigmoid_tanh(_dot(xb, wx_ref[...]) + bx_ref[...])
    log_a = (-LRU_C) * r * _softplus(-lam_ref[...])
    a = jnp.exp(log_a)
    a_scr[...] = a.reshape(tc, nb, LRU_WIDTH)
    b_scr[...] = (jnp.sqrt(1.0 - a * a) * (i * xc)).reshape(tc, nb, LRU_WIDTH)

    def step(s, h):
        t = (tc - 1 - s) if rev else s
        h = a_scr[t] * h + b_scr[t]
        for sl in range(LRU_WIDTH // LANES):
            slab_ref[sl, pl.ds(pl.multiple_of(t * nb, nb), nb), :] = h[:, sl * LANES:(sl + 1) * LANES]
        return h

    h = lax.fori_loop(0, tc, step, h_ref[...], unroll=8)
    h_ref[...] = h
    _slabs_to_block(slab_ref, o_ref, nb, tc)

    @pl.when(j == pl.num_programs(0) - 1)
    def _():
        hfin_ref[...] = h


def _lru(lx, cw, cb, wa, ba, wx, bx, lam, h0, rev, tc):
    b, l, _ = lx.shape
    nt = l // tc
    tmap = (lambda j: (0, nt - 1 - j, 0)) if rev else (lambda j: (0, j, 0))
    const2 = lambda j: (0, 0)
    w = LRU_WIDTH
    return pl.pallas_call(
        functools.partial(_lru_kernel, rev=rev, tc=tc),
        grid=(nt,),
        in_specs=[
            pl.BlockSpec((b, tc, w), tmap),
            pl.BlockSpec((LRU_CONV, w), const2),
            pl.BlockSpec((1, w), const2),
            pl.BlockSpec((w, w), const2),
            pl.BlockSpec((1, w), const2),
            pl.BlockSpec((w, w), const2),
            pl.BlockSpec((1, w), const2),
            pl.BlockSpec((1, w), const2),
            pl.BlockSpec((b, w), const2),
        ],
        out_specs=[pl.BlockSpec((b, tc, w), tmap), pl.BlockSpec((b, w), const2)],
        out_shape=[jax.ShapeDtypeStruct((b, l, w), F32), jax.ShapeDtypeStruct((b, w), F32)],
        scratch_shapes=[
            pltpu.VMEM((LRU_CONV - 1, b, w), F32),
            pltpu.VMEM((b, w), F32),
            pltpu.VMEM((tc, b, w), F32),
            pltpu.VMEM((tc, b, w), F32),
            pltpu.VMEM((w // LANES, tc * b, LANES), F32),
        ],
        compiler_params=pltpu.CompilerParams(
            dimension_semantics=("arbitrary",), vmem_limit_bytes=VMEM_LIMIT),
        name="lru_rev" if rev else "lru_fwd",
    )(lx, cw, cb, wa, ba, wx, bx, lam, h0)


def _s5_kernel(u_ref, bbar_ref, cmat_ref, ar_ref, ai_ref, h0_ref, o_ref, hfin_ref,
               st_ref, buf_a, buf_b, u_slab, y_slab, *, rev, tc):
    j = pl.program_id(0)
    nb = u_ref.shape[0]
    n = S5_LANES
    tw = S5_MXU_TILE
    ntile = 2 * n // tw
    spt = tc // ntile

    @pl.when(j == 0)
    def _():
        st_ref[...] = h0_ref[...]

    u = _to_time_major(u_ref, u_slab, nb, 2 * tc).reshape(2 * tc * nb, S5_WIDTH).astype(BF16)
    first, second = (1, 0) if rev else (0, 1)
    half_rows = lambda hf: slice(hf * tc * nb, (hf + 1) * tc * nb)
    ar = ar_ref[...]
    ai = ai_ref[...]

    def proj_in(buf, uh, g):
        cols = slice(g * tw, (g + 1) * tw)
        buf[:, :, cols] = _dot(uh, bbar_ref[:, cols]).reshape(tc, nb, tw)

    def scan_steps(buf, carry, g):
        xr, xi = carry
        for s in range(g * spt, (g + 1) * spt):
            t = (tc - 1 - s) if rev else s
            nxr = ar * xr - ai * xi + buf[t, :, 0:n]
            nxi = ar * xi + ai * xr + buf[t, :, n:2 * n]
            buf[t, :, 0:n] = nxr
            buf[t, :, n:2 * n] = nxi
            xr, xi = nxr, nxi
        return xr, xi

    def proj_out(buf, acc, g):
        cols = slice(g * tw, (g + 1) * tw)
        return acc + _dot(buf[:, :, cols].reshape(tc * nb, tw).astype(BF16), cmat_ref[cols, :])

    ua, ub = u[half_rows(first)], u[half_rows(second)]
    for g in range(ntile):
        proj_in(buf_a, ua, g)
    carry = (st_ref[0], st_ref[1])
    for g in range(ntile):
        carry = scan_steps(buf_a, carry, g)
        proj_in(buf_b, ub, g)
    ya = jnp.zeros((tc * nb, S5_WIDTH), F32)
    for g in range(ntile):
        carry = scan_steps(buf_b, carry, g)
        ya = proj_out(buf_a, ya, g)
    yb = jnp.zeros((tc * nb, S5_WIDTH), F32)
    for g in range(ntile):
        yb = proj_out(buf_b, yb, g)
    st_ref[0], st_ref[1] = carry

    for s in range(S5_WIDTH // LANES):
        y_slab[s, half_rows(first), :] = ya[:, s * LANES:(s + 1) * LANES]
        y_slab[s, half_rows(second), :] = yb[:, s * LANES:(s + 1) * LANES]
    _slabs_to_block(y_slab, o_ref, nb, 2 * tc)

    @pl.when(j == pl.num_programs(0) - 1)
    def _():
        hfin_ref[...] = st_ref[...]


def _s5(u_seq, bbar, cmat, ar, ai, h0, rev, tc):
    b, t, w = u_seq.shape
    n = S5_LANES
    nt = t // (2 * tc)
    tmap = (lambda j: (0, nt - 1 - j, 0)) if rev else (lambda j: (0, j, 0))
    const2 = lambda j: (0, 0)
    const3 = lambda j: (0, 0, 0)
    return pl.pallas_call(
        functools.partial(_s5_kernel, rev=rev, tc=tc),
        grid=(nt,),
        in_specs=[
            pl.BlockSpec((b, 2 * tc, w), tmap),
            pl.BlockSpec((w, 2 * n), const2),
            pl.BlockSpec((2 * n, w), const2),
            pl.BlockSpec((b, n), const2),
            pl.BlockSpec((b, n), const2),
            pl.BlockSpec((2, b, n), const3),
        ],
        out_specs=[pl.BlockSpec((b, 2 * tc, w), tmap), pl.BlockSpec((2, b, n), const3)],
        out_shape=[jax.ShapeDtypeStruct((b, t, w), F32), jax.ShapeDtypeStruct((2, b, n), F32)],
        scratch_shapes=[
            pltpu.VMEM((2, b, n), F32),
            pltpu.VMEM((tc, b, 2 * n), F32),
            pltpu.VMEM((tc, b, 2 * n), F32),
            pltpu.VMEM((w // LANES, 2 * tc * b, LANES), F32),
            pltpu.VMEM((w // LANES, 2 * tc * b, LANES), F32),
        ],
        compiler_params=pltpu.CompilerParams(
            dimension_semantics=("arbitrary",), vmem_limit_bytes=VMEM_LIMIT),
        name="s5_rev" if rev else "s5_fwd",
    )(u_seq, bbar, cmat, ar, ai, h0)


def _merge_mlp_kernel(x_ref, m_ref, gf_ref, gb_ref, gate_ref, lf_ref, lb_ref, sf_ref, sb_ref, su_ref,
                      gn_ref, sd_ref, gluw_ref, glub_ref, wout_ref, n2_ref, w1_ref, w2_ref, fn_ref,
                      o_ref, *, final, colmajor):
    tm = x_ref.shape[1]
    s_pre = sf_ref[0] + sb_ref[0] + sd_ref[...] * su_ref[0]
    if colmajor:
        s_pre = jnp.swapaxes(s_pre, 0, 1).reshape(TM_LAT, S5_WIDTH)

    def stages(rows):
        go = gf_ref[0, rows, :] + gb_ref[0, rows, :]
        heads = []
        for h in range(GLA_HEADS):
            oh = go[:, h * GLA_DV:(h + 1) * GLA_DV]
            heads.append(oh * lax.rsqrt(jnp.mean(oh * oh, axis=-1, keepdims=True) + EPS))
        gg = gate_ref[0, rows, 0:GLA_VAL]
        lg = gate_ref[0, rows, GLA_VAL:GLA_VAL + LRU_WIDTH]
        o = jnp.concatenate(heads, axis=-1) * gn_ref[...] * (gg * jax.nn.sigmoid(gg))
        r = (lf_ref[0, rows, :] + lb_ref[0, rows, :]) * _gelu_tanh(lg)
        s = _gelu_tanh(s_pre[rows])
        s = s * jax.nn.sigmoid(_dot(s.astype(BF16), gluw_ref[...]) + glub_ref[...])
        cat = jnp.concatenate([o, r, s], axis=-1).astype(BF16)
        yield
        x1 = x_ref[0, rows, :] + m_ref[0, 2:3, :] * _dot(cat, wout_ref[...])
        h2 = _rmsnorm_rows(x1, n2_ref[...]) * (1.0 + m_ref[0, 4:5, :]) + m_ref[0, 3:4, :]
        hb = h2.astype(BF16)
        yield
        acc = jnp.zeros_like(x1)
        for c in range(D_FF // FF_CHUNK):
            cs = slice(c * FF_CHUNK, (c + 1) * FF_CHUNK)
            t = jnp.maximum(_dot(hb, w1_ref[:, cs]), 0.0)
            acc = acc + _dot((t * t).astype(BF16), w2_ref[cs, :])
            yield
        x2 = x1 + m_ref[0, 5:6, :] * acc
        if final:
            x2 = _rmsnorm_rows(x2, fn_ref[...])
        o_ref[0, rows, :] = x2
        yield

    ngroup = MERGE_ROW_GROUPS if tm % (8 * MERGE_ROW_GROUPS) == 0 else 1
    gr = tm // ngroup
    gens = [stages(slice(g * gr, (g + 1) * gr)) for g in range(ngroup)]
    live = []
    while gens or live:
        if gens:
            live.append(gens.pop(0))
        for g in list(live):
            if next(g, StopIteration) is StopIteration:
                live.remove(g)


def _merge_mlp(x, mod, mod_per_batch, gla_f, gla_b, gate, lru_f, lru_b, s5_f, s5_b, su,
               gn, sd, gluw, glub, wout, n2, w1, w2, fn, tm, final, colmajor):
    b, l, d = x.shape
    tok = lambda w: pl.BlockSpec((1, tm, w), lambda i, j: (i, j, 0))
    if colmajor:
        assert tm == TM_LAT
        s5_tok = pl.BlockSpec((1, GRID_W, GRID_ROWS_PER_TILE, S5_WIDTH), lambda i, j: (i, 0, j, 0))
    else:
        s5_tok = tok(S5_WIDTH)
    mod_map = (lambda i, j: (i, 0, 0)) if mod_per_batch else (lambda i, j: (0, 0, 0))
    const = lambda shape: pl.BlockSpec(shape, lambda i, j: (0, 0), pipeline_mode=pl.Buffered(1))
    return pl.pallas_call(
        functools.partial(_merge_mlp_kernel, final=final, colmajor=colmajor),
        grid=(b, l // tm),
        in_specs=[
            tok(d),
            pl.BlockSpec((1, 6, d), mod_map),
            tok(GLA_VAL), tok(GLA_VAL), tok(GLA_VAL + LRU_WIDTH),
            tok(LRU_WIDTH), tok(LRU_WIDTH), s5_tok, s5_tok, s5_tok,
            const((1, GLA_VAL)), const((1, S5_WIDTH)), const((S5_WIDTH, S5_WIDTH)), const((1, S5_WIDTH)),
            const((d, d)), const((1, d)), const((d, D_FF)), const((D_FF, d)), const((1, d)),
        ],
        out_specs=tok(d),
        out_shape=jax.ShapeDtypeStruct((b, l, d), F32),
        compiler_params=pltpu.CompilerParams(
            dimension_semantics=("parallel", "parallel"), vmem_limit_bytes=VMEM_LIMIT),
        name="merge_mlp_final" if final else "merge_mlp",
    )(x, mod, gla_f, gla_b, gate, lru_f, lru_b, s5_f, s5_b, su, gn, sd, gluw, glub, wout, n2, w1, w2, fn)


def _block_diag(blocks):
    n, r, c = blocks.shape
    eye = jnp.eye(n, dtype=blocks.dtype)
    return (eye[:, None, :, None] * blocks[:, :, None, :]).reshape(n * r, n * c)


def _pack_w_in(w):
    d = w.shape[0]
    q_k_v_gg = w[:, 0:1536]
    lr = w[:, 1536:1568]
    lx = w[:, 1568:1824]
    lg = w[:, 1824:2080]
    su = w[:, 2080:2336]
    pad = jnp.zeros((d, LR_PAD - lr.shape[1]), w.dtype)
    return jnp.concatenate([q_k_v_gg, lg, lr, pad, lx, su], axis=1).astype(BF16)


def _s5_params(lam_re, lam_im, log_dt, b_re, b_im, c_re, c_im, nb):
    dt = jnp.exp(log_dt)[:, None]
    mag = jnp.exp(lam_re * dt)
    ang = lam_im * dt
    abar_r, abar_i = mag * jnp.cos(ang), mag * jnp.sin(ang)
    den = lam_re * lam_re + lam_im * lam_im
    num_r = abar_r - 1.0
    coef_r = (num_r * lam_re + abar_i * lam_im) / den
    coef_i = (abar_i * lam_re - num_r * lam_im) / den
    bbar_r = coef_r[..., None] * b_re - coef_i[..., None] * b_im
    bbar_i = coef_r[..., None] * b_im + coef_i[..., None] * b_re
    bd_in = lambda m: _block_diag(jnp.swapaxes(m, 1, 2))
    bbar = jnp.concatenate([bd_in(bbar_r), bd_in(bbar_i)], axis=1).astype(BF16)
    bd_out = lambda m: _block_diag(jnp.swapaxes(m, 1, 2))
    cmat = jnp.concatenate([bd_out(c_re), -bd_out(c_im)], axis=0).astype(BF16)
    ar = jnp.broadcast_to(abar_r.reshape(1, S5_LANES), (nb, S5_LANES))
    ai = jnp.broadcast_to(abar_i.reshape(1, S5_LANES), (nb, S5_LANES))
    return bbar, cmat, ar, ai


def _tile(n, pref):
    t = min(n, pref)
    while n % t:
        t //= 2
    return t


def kernel(x, c, ctx, c_ctx, w_mod, b_mod, norm1, norm2, w_in, gla_up_w, gla_up_b, gla_norm, lru_conv_w, lru_conv_b, lru_wa, lru_ba, lru_wx, lru_bx, lru_lambda, s5_lam_re, s5_lam_im, s5_log_dt, s5_b_re, s5_b_im, s5_c_re, s5_c_im, s5_d, s5_glu_w, s5_glu_b, w_out, w_ff1, w_ff2, final_norm):
    bsz, seq, d = x.shape
    ctx_len = ctx.shape[1]
    depth = w_mod.shape[0]
    rows = seq // GRID_W
    assert d == D_MODEL and seq == rows * GRID_W and rows % GRID_ROWS_PER_TILE == 0
    assert seq % (2 * S5_CHUNK) == 0 and ctx_len % (2 * S5_CHUNK) == 0

    cc = jnp.concatenate([c, c_ctx[None, :], jnp.zeros((7, d), F32)], axis=0)
    mod_all = _modulation(cc, w_mod, b_mod).reshape(depth, bsz + 8, 6, d)

    tm_ctx = _tile(ctx_len, 512)
    tl_lat, tl_ctx = _tile(seq, 1024), _tile(ctx_len, 1024)
    tc_lat, tc_ctx = _tile(seq, 256), _tile(ctx_len, 256)
    row = lambda v: v.reshape(1, -1)
    cm_view = lambda t: t.reshape(bsz, GRID_W, rows, S5_WIDTH)

    x_lat, x_ctx = x, ctx
    for l in range(depth):
        last = l == depth - 1
        mod_lat, mod_ctx = mod_all[l, :bsz], mod_all[l, bsz:bsz + 1]
        w_packed = _pack_w_in(w_in[l])
        qkv_l, gate_l, lr_l, lx_l, su_l = _inproj(x_lat, mod_lat, True, row(norm1[l]), w_packed, TM_LAT, True)
        qkv_c, gate_c, lr_c, lx_c, su_c = _inproj(x_ctx, mod_ctx, False, row(norm1[l]), w_packed, tm_ctx, False)
        su_seq = su_l.reshape(bsz, seq, S5_WIDTH)

        mix_l, mix_c = [], []
        for dr in range(N_DIR):
            rev = dr == 1
            upw = jnp.zeros((LR_PAD, GLA_KEY), F32).at[dr * GLA_RANK:(dr + 1) * GLA_RANK].set(gla_up_w[l, dr]).astype(BF16)
            upb = row(gla_up_b[l, dr])
            s0 = jnp.zeros((bsz, GLA_HEADS, GLA_DV, GLA_DK), F32)
            g_c, s_c = _gla(qkv_c, lr_c, upw, upb, s0, rev, tl_ctx)
            g_l, _ = _gla(qkv_l, lr_l, upw, upb, s_c, rev, tl_lat)
            lru_args = (lru_conv_w[l, dr], row(lru_conv_b[l, dr]),
                        _block_diag(lru_wa[l, dr]).astype(BF16), row(lru_ba[l, dr]),
                        _block_diag(lru_wx[l, dr]).astype(BF16), row(lru_bx[l, dr]),
                        row(lru_lambda[l, dr]))
            r_c, h_c = _lru(lx_c, *lru_args, jnp.zeros((bsz, LRU_WIDTH), F32), rev, tc_ctx)
            r_l, _ = _lru(lx_l, *lru_args, h_c, rev, tc_lat)
            s5_args = _s5_params(s5_lam_re[l, dr], s5_lam_im[l, dr], s5_log_dt[l, dr], s5_b_re[l, dr],
                                 s5_b_im[l, dr], s5_c_re[l, dr], s5_c_im[l, dr], bsz)
            y_c, x_c = _s5(su_c, *s5_args, jnp.zeros((2, bsz, S5_LANES), F32), rev, S5_CHUNK)
            y_l, _ = _s5(su_seq, *s5_args, x_c, rev, S5_CHUNK)
            mix_l.append((g_l, r_l, cm_view(y_l)))
            mix_c.append((g_c, r_c, y_c))

        merge_w = (row(gla_norm[l]), row(s5_d[l]), s5_glu_w[l].astype(BF16), row(s5_glu_b[l]),
                   w_out[l].astype(BF16), row(norm2[l]), w_ff1[l].astype(BF16), w_ff2[l].astype(BF16),
                   row(final_norm))
        (gf, lf, sf), (gb, lb, sb) = mix_l
        x_lat = _merge_mlp(x_lat, mod_lat, True, gf, gb, gate_l, lf, lb, sf, sb, su_l, *merge_w,
                           TM_LAT, last, True)
        if not last:
            (gf, lf, sf), (gb, lb, sb) = mix_c
            x_ctx = _merge_mlp(x_ctx, mod_ctx, False, gf, gb, gate_c, lf, lb, sf, sb, su_c, *merge_w,
                               tm_ctx, False, False)
    return x_lat
```

```python
import functools

import jax
import jax.numpy as jnp
from jax import lax
from jax.experimental import pallas as pl
from jax.experimental.pallas import tpu as pltpu

F32 = jnp.float32
BF16 = jnp.bfloat16

LANES = 128
D_MODEL = 1024
GRID_W = 64
N_DIR = 2
EPS = 1e-6

GLA_HEADS = 4
GLA_VAL = 512
GLA_KEY = 256
GLA_DV = 128
GLA_DK = 64
GLA_RANK = 16
GLA_TAU = 16.0
GLA_CHUNK = 64
GLA_PAIR = 2 * GLA_CHUNK

LRU_WIDTH = 256
LRU_BLOCKS = 4
LRU_BLOCK = 64
LRU_CONV = 4
LRU_C = 8.0

S5_WIDTH = 256
S5_GROUP = 16
S5_GROUPS = 16
S5_STATE = 64
S5_LANES = S5_GROUPS * S5_STATE
S5_MXU_TILE = 256
S5_CHUNK = 64
S5_CHUNKS_PER_STEP = 4

D_FF = 4 * D_MODEL
FF_CHUNK = 1024

LR_PAD = 128
COL_QKV = (0, 1024)
COL_GATE = (1024, 1792)
COL_LR = (1792, 1920)
COL_LX = (1920, 2176)
COL_SU = (2176, 2432)
IN_PACKED = 2432
GRID_ROWS_PER_TILE = 8
TM_LAT = GRID_ROWS_PER_TILE * GRID_W

VMEM_LIMIT = 56 * 1024 * 1024


def _rmsnorm_rows(x, g):
    ms = jnp.mean(x * x, axis=-1, keepdims=True)
    return x * lax.rsqrt(ms + EPS) * g


def _gelu_tanh(x):
    c = 0.7978845608028654
    return 0.5 * x * (1.0 + jnp.tanh(c * (x + 0.044715 * (x * x * x))))


def _softplus(x):
    return jnp.maximum(x, 0.0) + jnp.log1p(jnp.exp(-jnp.abs(x)))


def _dot(a, b):
    return jnp.dot(a, b, preferred_element_type=F32)


def _mod_kernel(c_ref, w_ref, b_ref, o_ref):
    cs = c_ref[...]
    s = cs * jax.nn.sigmoid(cs)
    o_ref[0] = _dot(s.astype(BF16), w_ref[0].astype(BF16)) + b_ref[0]


def _modulation(cc, w_mod, b_mod):
    depth, d, n = w_mod.shape
    rows = cc.shape[0]
    tn = 1536
    return pl.pallas_call(
        _mod_kernel,
        grid=(depth, n // tn),
        in_specs=[
            pl.BlockSpec((rows, d), lambda l, j: (0, 0)),
            pl.BlockSpec((1, d, tn), lambda l, j: (l, 0, j)),
            pl.BlockSpec((1, 1, tn), lambda l, j: (l, 0, j)),
        ],
        out_specs=pl.BlockSpec((1, rows, tn), lambda l, j: (l, 0, j)),
        out_shape=jax.ShapeDtypeStruct((depth, rows, n), F32),
        compiler_params=pltpu.CompilerParams(
            dimension_semantics=("parallel", "parallel"), vmem_limit_bytes=VMEM_LIMIT),
        name="modulation",
    )(cc, w_mod, b_mod.reshape(depth, 1, n))


def _inproj_kernel(x_ref, m_ref, g_ref, w_ref, qkv_ref, gate_ref, lr_ref, lx_ref, su_ref, *, colmajor):
    h = _rmsnorm_rows(x_ref[0], g_ref[...]) * (1.0 + m_ref[0, 1:2, :]) + m_ref[0, 0:1, :]
    hb = h.astype(BF16)
    for ref, (c0, c1) in ((qkv_ref, COL_QKV), (gate_ref, COL_GATE), (lr_ref, COL_LR), (lx_ref, COL_LX)):
        ref[0] = _dot(hb, w_ref[:, c0:c1])
    su = _dot(hb, w_ref[:, COL_SU[0]:COL_SU[1]])
    if colmajor:
        su_ref[0] = jnp.swapaxes(su.reshape(GRID_ROWS_PER_TILE, GRID_W, S5_WIDTH), 0, 1)
    else:
        su_ref[0] = su


def _inproj(x, mod, mod_per_batch, norm_g, w_packed, tm, colmajor):
    b, l, d = x.shape
    widths = [c1 - c0 for c0, c1 in (COL_QKV, COL_GATE, COL_LR, COL_LX)]
    mod_map = (lambda i, j: (i, 0, 0)) if mod_per_batch else (lambda i, j: (0, 0, 0))
    tok = lambda w: pl.BlockSpec((1, tm, w), lambda i, j: (i, j, 0))
    if colmajor:
        assert tm == TM_LAT
        su_spec = pl.BlockSpec((1, GRID_W, GRID_ROWS_PER_TILE, S5_WIDTH), lambda i, j: (i, 0, j, 0))
        su_shape = jax.ShapeDtypeStruct((b, GRID_W, l // GRID_W, S5_WIDTH), F32)
    else:
        su_spec, su_shape = tok(S5_WIDTH), jax.ShapeDtypeStruct((b, l, S5_WIDTH), F32)
    return pl.pallas_call(
        functools.partial(_inproj_kernel, colmajor=colmajor),
        grid=(b, l // tm),
        in_specs=[
            tok(d),
            pl.BlockSpec((1, 6, d), mod_map),
            pl.BlockSpec((1, d), lambda i, j: (0, 0)),
            pl.BlockSpec((d, IN_PACKED), lambda i, j: (0, 0)),
        ],
        out_specs=[tok(w) for w in widths] + [su_spec],
        out_shape=[jax.ShapeDtypeStruct((b, l, w), F32) for w in widths] + [su_shape],
        compiler_params=pltpu.CompilerParams(
            dimension_semantics=("parallel", "parallel"), vmem_limit_bytes=VMEM_LIMIT),
        name="inproj",
    )(x, mod, norm_g, w_packed)


def _gla_kernel(qkv_ref, lr_ref, upw_ref, upb_ref, s0_ref, o_ref, sfin_ref, st_ref, *, rev, tl):
    j = pl.program_id(1)

    @pl.when(j == 0)
    def _():
        st_ref[...] = s0_ref[0]

    c = GLA_CHUNK
    pr = GLA_PAIR
    nchunk = tl // c
    npair = tl // pr
    row = lax.broadcasted_iota(jnp.int32, (pr, pr), 0)
    col = lax.broadcasted_iota(jnp.int32, (pr, pr), 1)
    keep = ((row // c) == (col // c)) & ((col >= row) if rev else (col <= row))
    tri = keep.astype(BF16)
    nt_dims = (((1,), (1,)), ((), ()))
    tn_dims = (((0,), (0,)), ((), ()))
    order = (lambda n: range(n - 1, -1, -1)) if rev else range
    crow = lambda ci: slice(ci * c, (ci + 1) * c)
    prow = lambda pi: slice(pi * pr, (pi + 1) * pr)
    kcol = lambda h: slice(h * GLA_DK, (h + 1) * GLA_DK)
    vcol = lambda h: slice(h * GLA_DV, (h + 1) * GLA_DV)
    heads = range(GLA_HEADS)

    logit = _dot(lr_ref[0].astype(BF16), upw_ref[...]) + upb_ref[...]
    log_a = (jnp.minimum(logit, 0.0) - jnp.log(1.0 + jnp.exp(-jnp.abs(logit)))) * (1.0 / GLA_TAU)
    hi = log_a.astype(BF16)
    lo = (log_a - hi.astype(F32)).astype(BF16)
    bcum = jnp.concatenate([_dot(tri, hi[prow(pi)]) + _dot(tri, lo[prow(pi)]) for pi in range(npair)], axis=0)
    blast = [bcum[ci * c:ci * c + 1, :] if rev else bcum[(ci + 1) * c - 1:(ci + 1) * c, :]
             for ci in range(nchunk)]
    btot = jnp.concatenate([jnp.broadcast_to(bl, (c, GLA_KEY)) for bl in blast], axis=0)
    q = qkv_ref[0, :, 0:GLA_KEY]
    k = qkv_ref[0, :, GLA_KEY:2 * GLA_KEY]
    v = qkv_ref[0, :, 2 * GLA_KEY:2 * GLA_KEY + GLA_VAL].astype(BF16)
    q_dec = ((q * (GLA_DK ** -0.5)) * jnp.exp(bcum)).astype(BF16)
    k_inv = (k * jnp.exp(-bcum)).astype(BF16)
    k_end = (k * jnp.exp(btot - bcum)).astype(BF16)
    decay = [jnp.exp(bl) for bl in blast]
    sc = {(pi, h): lax.dot_general(q_dec[prow(pi), kcol(h)], k_inv[prow(pi), kcol(h)], nt_dims,
                                   preferred_element_type=F32)
          for pi in range(npair) for h in heads}
    scm = {key: jnp.where(keep, val, 0.0).astype(BF16) for key, val in sc.items()}
    o_intra = {(pi, h): _dot(scm[pi, h], v[prow(pi), vcol(h)]) for pi in range(npair) for h in heads}
    ds = {(ci, h): lax.dot_general(v[crow(ci), vcol(h)], k_end[crow(ci), kcol(h)], tn_dims,
                                   preferred_element_type=F32)
          for ci in range(nchunk) for h in heads}
    s_in = {}
    for h in heads:
        st = st_ref[h]
        for ci in order(nchunk):
            s_in[ci, h] = st.astype(BF16)
            st = st * decay[ci][:, kcol(h)] + ds[ci, h]
        st_ref[h] = st
    o_inter = {(ci, h): lax.dot_general(q_dec[crow(ci), kcol(h)], s_in[ci, h], nt_dims,
                                        preferred_element_type=F32)
               for ci in range(nchunk) for h in heads}
    for ci in range(nchunk):
        half = slice((ci % 2) * c, (ci % 2 + 1) * c)
        o_ref[0, crow(ci), :] = jnp.concatenate(
            [o_intra[ci // 2, h][half] + o_inter[ci, h] for h in heads], axis=-1)

    @pl.when(j == pl.num_programs(1) - 1)
    def _():
        sfin_ref[0] = st_ref[...]


def _gla(qkv, lr, upw, upb, s0, rev, tl):
    b, l, _ = qkv.shape
    nt = l // tl
    tmap = (lambda i, j: (i, nt - 1 - j, 0)) if rev else (lambda i, j: (i, j, 0))
    state_shape = (b, GLA_HEADS, GLA_DV, GLA_DK)
    return pl.pallas_call(
        functools.partial(_gla_kernel, rev=rev, tl=tl),
        grid=(b, nt),
        in_specs=[
            pl.BlockSpec((1, tl, 2 * GLA_KEY + GLA_VAL), tmap),
            pl.BlockSpec((1, tl, LR_PAD), tmap),
            pl.BlockSpec((LR_PAD, GLA_KEY), lambda i, j: (0, 0)),
            pl.BlockSpec((1, GLA_KEY), lambda i, j: (0, 0)),
            pl.BlockSpec((1,) + state_shape[1:], lambda i, j: (i, 0, 0, 0)),
        ],
        out_specs=[
            pl.BlockSpec((1, tl, GLA_VAL), tmap),
            pl.BlockSpec((1,) + state_shape[1:], lambda i, j: (i, 0, 0, 0)),
        ],
        out_shape=[jax.ShapeDtypeStruct((b, l, GLA_VAL), F32), jax.ShapeDtypeStruct(state_shape, F32)],
        scratch_shapes=[pltpu.VMEM(state_shape[1:], F32)],
        compiler_params=pltpu.CompilerParams(
            dimension_semantics=("parallel", "arbitrary"), vmem_limit_bytes=VMEM_LIMIT),
        name="gla_rev" if rev else "gla_fwd",
    )(qkv, lr, upw, upb, s0)


def _to_time_major(src_ref, slab_ref, nb, nt):
    nslab = slab_ref.shape[0]
    for b in range(nb):
        for s in range(nslab):
            slab_ref[s, pl.ds(b, nt, stride=nb), :] = src_ref[b, :, s * LANES:(s + 1) * LANES]
    flat = jnp.concatenate([slab_ref[s] for s in range(nslab)], axis=-1)
    return flat.reshape(nt, nb, nslab * LANES)


def _slabs_to_block(slab_ref, dst_ref, nb, nt):
    for b in range(nb):
        for s in range(slab_ref.shape[0]):
            dst_ref[b, :, s * LANES:(s + 1) * LANES] = slab_ref[s, pl.ds(b, nt, stride=nb), :]


def _lru_kernel(x_ref, cw_ref, cb_ref, wa_ref, ba_ref, wx_ref, bx_ref, lam_ref, h0_ref,
                o_ref, hfin_ref, halo_ref, h_ref, a_scr, b_scr, slab_ref, *, rev, tc):
    j = pl.program_id(0)
    nb = x_ref.shape[0]
    taps = LRU_CONV - 1

    @pl.when(j == 0)
    def _():
        halo_ref[...] = jnp.zeros_like(halo_ref)
        h_ref[...] = h0_ref[...]

    xt = _to_time_major(x_ref, slab_ref, nb, tc)
    cw = cw_ref[...]
    if rev:
        ext = jnp.concatenate([xt, halo_ref[...]], axis=0)
        xc = sum(cw[k:k + 1, :] * ext[taps - k:taps - k + tc] for k in range(LRU_CONV))
        halo_ref[...] = xt[:taps]
    else:
        ext = jnp.concatenate([halo_ref[...], xt], axis=0)
        xc = sum(cw[k:k + 1, :] * ext[k:k + tc] for k in range(LRU_CONV))
        halo_ref[...] = xt[tc - taps:]
    xc = (xc + cb_ref[...]).reshape(tc * nb, LRU_WIDTH)
    xb = xc.astype(BF16)
    r = jax.nn.sigmoid(_dot(xb, wa_ref[...]) + ba_ref[...])
    i = jax.nn.sigmoid(_dot(xb, wx_ref[...]) + bx_ref[...])
    log_a = (-LRU_C) * r * _softplus(-lam_ref[...])
    a = jnp.exp(log_a)
    a_scr[...] = a.reshape(tc, nb, LRU_WIDTH)
    b_scr[...] = (jnp.sqrt(1.0 - a * a) * (i * xc)).reshape(tc, nb, LRU_WIDTH)

    def step(s, h):
        t = (tc - 1 - s) if rev else s
        h = a_scr[t] * h + b_scr[t]
        for sl in range(LRU_WIDTH // LANES):
            slab_ref[sl, pl.ds(pl.multiple_of(t * nb, nb), nb), :] = h[:, sl * LANES:(sl + 1) * LANES]
        return h

    h = lax.fori_loop(0, tc, step, h_ref[...], unroll=8)
    h_ref[...] = h
    _slabs_to_block(slab_ref, o_ref, nb, tc)

    @pl.when(j == pl.num_programs(0) - 1)
    def _():
        hfin_ref[...] = h


def _lru(lx, cw, cb, wa, ba, wx, bx, lam, h0, rev, tc):
    b, l, _ = lx.shape
    nt = l // tc
    tmap = (lambda j: (0, nt - 1 - j, 0)) if rev else (lambda j: (0, j, 0))
    const2 = lambda j: (0, 0)
    w = LRU_WIDTH
    return pl.pallas_call(
        functools.partial(_lru_kernel, rev=rev, tc=tc),
        grid=(nt,),
        in_specs=[
            pl.BlockSpec((b, tc, w), tmap),
            pl.BlockSpec((LRU_CONV, w), const2),
            pl.BlockSpec((1, w), const2),
            pl.BlockSpec((w, w), const2),
            pl.BlockSpec((1, w), const2),
            pl.BlockSpec((w, w), const2),
            pl.BlockSpec((1, w), const2),
            pl.BlockSpec((1, w), const2),
            pl.BlockSpec((b, w), const2),
        ],
        out_specs=[pl.BlockSpec((b, tc, w), tmap), pl.BlockSpec((b, w), const2)],
        out_shape=[jax.ShapeDtypeStruct((b, l, w), F32), jax.ShapeDtypeStruct((b, w), F32)],
        scratch_shapes=[
            pltpu.VMEM((LRU_CONV - 1, b, w), F32),
            pltpu.VMEM((b, w), F32),
            pltpu.VMEM((tc, b, w), F32),
            pltpu.VMEM((tc, b, w), F32),
            pltpu.VMEM((w // LANES, tc * b, LANES), F32),
        ],
        compiler_params=pltpu.CompilerParams(
            dimension_semantics=("arbitrary",), vmem_limit_bytes=VMEM_LIMIT),
        name="lru_rev" if rev else "lru_fwd",
    )(lx, cw, cb, wa, ba, wx, bx, lam, h0)


def _s5_kernel(u_ref, bbar_ref, cmat_ref, ar_ref, ai_ref, h0_ref, o_ref, hfin_ref,
               st_ref, buf_a, buf_b, u_slab, y_slab, *, rev, tc):
    j = pl.program_id(0)
    nb = u_ref.shape[0]
    n = S5_LANES
    tw = S5_MXU_TILE
    ntile = 2 * n // tw
    spt = tc // ntile

    @pl.when(j == 0)
    def _():
        st_ref[...] = h0_ref[...]

    nck = S5_CHUNKS_PER_STEP
    u = _to_time_major(u_ref, u_slab, nb, nck * tc).reshape(nck * tc * nb, S5_WIDTH).astype(BF16)
    order = list(range(nck - 1, -1, -1)) if rev else list(range(nck))
    chunk_rows = lambda ck: slice(ck * tc * nb, (ck + 1) * tc * nb)
    ar = ar_ref[...]
    ai = ai_ref[...]

    def proj_in(buf, uh, g):
        cols = slice(g * tw, (g + 1) * tw)
        buf[:, :, cols] = _dot(uh, bbar_ref[:, cols]).reshape(tc, nb, tw)

    def scan_steps(buf, carry, g):
        xr, xi = carry
        for s in range(g * spt, (g + 1) * spt):
            t = (tc - 1 - s) if rev else s
            nxr = ar * xr - ai * xi + buf[t, :, 0:n]
            nxi = ar * xi + ai * xr + buf[t, :, n:2 * n]
            buf[t, :, 0:n] = nxr
            buf[t, :, n:2 * n] = nxi
            xr, xi = nxr, nxi
        return xr, xi

    def proj_out(buf, acc, g):
        cols = slice(g * tw, (g + 1) * tw)
        return acc + _dot(buf[:, :, cols].reshape(tc * nb, tw).astype(BF16), cmat_ref[cols, :])

    bufs = (buf_a, buf_b)
    uk = [u[chunk_rows(ck)] for ck in order]
    ys = [jnp.zeros((tc * nb, S5_WIDTH), F32) for _ in order]
    for g in range(ntile):
        proj_in(bufs[0], uk[0], g)
    carry = (st_ref[0], st_ref[1])
    for k in range(nck):
        cur, other = bufs[k % 2], bufs[(k + 1) % 2]
        for g in range(ntile):
            carry = scan_steps(cur, carry, g)
            if k >= 1:
                ys[k - 1] = proj_out(other, ys[k - 1], g)
            if k + 1 < nck:
                proj_in(other, uk[k + 1], g)
    for g in range(ntile):
        ys[nck - 1] = proj_out(bufs[(nck - 1) % 2], ys[nck - 1], g)
    st_ref[0], st_ref[1] = carry

    for s in range(S5_WIDTH // LANES):
        for k, ck in enumerate(order):
            y_slab[s, chunk_rows(ck), :] = ys[k][:, s * LANES:(s + 1) * LANES]
    _slabs_to_block(y_slab, o_ref, nb, nck * tc)

    @pl.when(j == pl.num_programs(0) - 1)
    def _():
        hfin_ref[...] = st_ref[...]


def _s5(u_seq, bbar, cmat, ar, ai, h0, rev, tc):
    b, t, w = u_seq.shape
    n = S5_LANES
    ts = S5_CHUNKS_PER_STEP * tc
    nt = t // ts
    tmap = (lambda j: (0, nt - 1 - j, 0)) if rev else (lambda j: (0, j, 0))
    const2 = lambda j: (0, 0)
    const3 = lambda j: (0, 0, 0)
    return pl.pallas_call(
        functools.partial(_s5_kernel, rev=rev, tc=tc),
        grid=(nt,),
        in_specs=[
            pl.BlockSpec((b, ts, w), tmap),
            pl.BlockSpec((w, 2 * n), const2),
            pl.BlockSpec((2 * n, w), const2),
            pl.BlockSpec((b, n), const2),
            pl.BlockSpec((b, n), const2),
            pl.BlockSpec((2, b, n), const3),
        ],
        out_specs=[pl.BlockSpec((b, ts, w), tmap), pl.BlockSpec((2, b, n), const3)],
        out_shape=[jax.ShapeDtypeStruct((b, t, w), F32), jax.ShapeDtypeStruct((2, b, n), F32)],
        scratch_shapes=[
            pltpu.VMEM((2, b, n), F32),
            pltpu.VMEM((tc, b, 2 * n), F32),
            pltpu.VMEM((tc, b, 2 * n), F32),
            pltpu.VMEM((w // LANES, ts * b, LANES), F32),
            pltpu.VMEM((w // LANES, ts * b, LANES), F32),
        ],
        compiler_params=pltpu.CompilerParams(
            dimension_semantics=("arbitrary",), vmem_limit_bytes=VMEM_LIMIT),
        name="s5_rev" if rev else "s5_fwd",
    )(u_seq, bbar, cmat, ar, ai, h0)


def _merge_mlp_kernel(x_ref, m_ref, gf_ref, gb_ref, gate_ref, lf_ref, lb_ref, sf_ref, sb_ref, su_ref,
                      gn_ref, sd_ref, gluw_ref, glub_ref, wout_ref, n2_ref, w1_ref, w2_ref, fn_ref,
                      o_ref, *, final, colmajor):
    x = x_ref[0]
    go = gf_ref[0] + gb_ref[0]
    heads = []
    for h in range(GLA_HEADS):
        oh = go[:, h * GLA_DV:(h + 1) * GLA_DV]
        heads.append(oh * lax.rsqrt(jnp.mean(oh * oh, axis=-1, keepdims=True) + EPS))
    gg = gate_ref[0, :, 0:GLA_VAL]
    lg = gate_ref[0, :, GLA_VAL:GLA_VAL + LRU_WIDTH]
    o = jnp.concatenate(heads, axis=-1) * gn_ref[...] * (gg * jax.nn.sigmoid(gg))
    r = (lf_ref[0] + lb_ref[0]) * _gelu_tanh(lg)
    s = sf_ref[0] + sb_ref[0] + sd_ref[...] * su_ref[0]
    if colmajor:
        s = jnp.swapaxes(s, 0, 1).reshape(TM_LAT, S5_WIDTH)
    s = _gelu_tanh(s)
    s = s * jax.nn.sigmoid(_dot(s.astype(BF16), gluw_ref[...]) + glub_ref[...])
    cat = jnp.concatenate([o, r, s], axis=-1).astype(BF16)
    x1 = x + m_ref[0, 2:3, :] * _dot(cat, wout_ref[...])
    h2 = _rmsnorm_rows(x1, n2_ref[...]) * (1.0 + m_ref[0, 4:5, :]) + m_ref[0, 3:4, :]
    hb = h2.astype(BF16)
    acc = jnp.zeros_like(x1)
    for c in range(D_FF // FF_CHUNK):
        cs = slice(c * FF_CHUNK, (c + 1) * FF_CHUNK)
        t = jnp.maximum(_dot(hb, w1_ref[:, cs]), 0.0)
        acc = acc + _dot((t * t).astype(BF16), w2_ref[cs, :])
    x2 = x1 + m_ref[0, 5:6, :] * acc
    if final:
        x2 = _rmsnorm_rows(x2, fn_ref[...])
    o_ref[0] = x2


def _merge_mlp(x, mod, mod_per_batch, gla_f, gla_b, gate, lru_f, lru_b, s5_f, s5_b, su,
               gn, sd, gluw, glub, wout, n2, w1, w2, fn, tm, final, colmajor):
    b, l, d = x.shape
    tok = lambda w: pl.BlockSpec((1, tm, w), lambda i, j: (i, j, 0))
    if colmajor:
        assert tm == TM_LAT
        s5_tok = pl.BlockSpec((1, GRID_W, GRID_ROWS_PER_TILE, S5_WIDTH), lambda i, j: (i, 0, j, 0))
    else:
        s5_tok = tok(S5_WIDTH)
    mod_map = (lambda i, j: (i, 0, 0)) if mod_per_batch else (lambda i, j: (0, 0, 0))
    const = lambda shape: pl.BlockSpec(shape, lambda i, j: (0, 0), pipeline_mode=pl.Buffered(1))
    return pl.pallas_call(
        functools.partial(_merge_mlp_kernel, final=final, colmajor=colmajor),
        grid=(b, l // tm),
        in_specs=[
            tok(d),
            pl.BlockSpec((1, 6, d), mod_map),
            tok(GLA_VAL), tok(GLA_VAL), tok(GLA_VAL + LRU_WIDTH),
            tok(LRU_WIDTH), tok(LRU_WIDTH), s5_tok, s5_tok, s5_tok,
            const((1, GLA_VAL)), const((1, S5_WIDTH)), const((S5_WIDTH, S5_WIDTH)), const((1, S5_WIDTH)),
            const((d, d)), const((1, d)), const((d, D_FF)), const((D_FF, d)), const((1, d)),
        ],
        out_specs=tok(d),
        out_shape=jax.ShapeDtypeStruct((b, l, d), F32),
        compiler_params=pltpu.CompilerParams(
            dimension_semantics=("parallel", "parallel"), vmem_limit_bytes=VMEM_LIMIT),
        name="merge_mlp_final" if final else "merge_mlp",
    )(x, mod, gla_f, gla_b, gate, lru_f, lru_b, s5_f, s5_b, su, gn, sd, gluw, glub, wout, n2, w1, w2, fn)


def _block_diag(blocks):
    n, r, c = blocks.shape
    eye = jnp.eye(n, dtype=blocks.dtype)
    return (eye[:, None, :, None] * blocks[:, :, None, :]).reshape(n * r, n * c)


def _pack_w_in(w):
    d = w.shape[0]
    q_k_v_gg = w[:, 0:1536]
    lr = w[:, 1536:1568]
    lx = w[:, 1568:1824]
    lg = w[:, 1824:2080]
    su = w[:, 2080:2336]
    pad = jnp.zeros((d, LR_PAD - lr.shape[1]), w.dtype)
    return jnp.concatenate([q_k_v_gg, lg, lr, pad, lx, su], axis=1).astype(BF16)


def _s5_params(lam_re, lam_im, log_dt, b_re, b_im, c_re, c_im, nb):
    dt = jnp.exp(log_dt)[:, None]
    mag = jnp.exp(lam_re * dt)
    ang = lam_im * dt
    abar_r, abar_i = mag * jnp.cos(ang), mag * jnp.sin(ang)
    den = lam_re * lam_re + lam_im * lam_im
    num_r = abar_r - 1.0
    coef_r = (num_r * lam_re + abar_i * lam_im) / den
    coef_i = (abar_i * lam_re - num_r * lam_im) / den
    bbar_r = coef_r[..., None] * b_re - coef_i[..., None] * b_im
    bbar_i = coef_r[..., None] * b_im + coef_i[..., None] * b_re
    bd_in = lambda m: _block_diag(jnp.swapaxes(m, 1, 2))
    bbar = jnp.concatenate([bd_in(bbar_r), bd_in(bbar_i)], axis=1).astype(BF16)
    bd_out = lambda m: _block_diag(jnp.swapaxes(m, 1, 2))
    cmat = jnp.concatenate([bd_out(c_re), -bd_out(c_im)], axis=0).astype(BF16)
    ar = jnp.broadcast_to(abar_r.reshape(1, S5_LANES), (nb, S5_LANES))
    ai = jnp.broadcast_to(abar_i.reshape(1, S5_LANES), (nb, S5_LANES))
    return bbar, cmat, ar, ai


def _tile(n, pref):
    t = min(n, pref)
    while n % t:
        t //= 2
    return t


def kernel(x, c, ctx, c_ctx, w_mod, b_mod, norm1, norm2, w_in, gla_up_w, gla_up_b, gla_norm, lru_conv_w, lru_conv_b, lru_wa, lru_ba, lru_wx, lru_bx, lru_lambda, s5_lam_re, s5_lam_im, s5_log_dt, s5_b_re, s5_b_im, s5_c_re, s5_c_im, s5_d, s5_glu_w, s5_glu_b, w_out, w_ff1, w_ff2, final_norm):
    bsz, seq, d = x.shape
    ctx_len = ctx.shape[1]
    depth = w_mod.shape[0]
    rows = seq // GRID_W
    assert d == D_MODEL and seq == rows * GRID_W and rows % GRID_ROWS_PER_TILE == 0
    s5_steps = S5_CHUNKS_PER_STEP * S5_CHUNK
    assert seq % s5_steps == 0 and ctx_len % s5_steps == 0 and ctx_len % GLA_PAIR == 0

    cc = jnp.concatenate([c, c_ctx[None, :], jnp.zeros((7, d), F32)], axis=0)
    mod_all = _modulation(cc, w_mod, b_mod).reshape(depth, bsz + 8, 6, d)

    tm_ctx = _tile(ctx_len, 512)
    tl_lat, tl_ctx = _tile(seq, 1024), _tile(ctx_len, 1024)
    tc_lat, tc_ctx = _tile(seq, 256), _tile(ctx_len, 256)
    row = lambda v: v.reshape(1, -1)
    cm_view = lambda t: t.reshape(bsz, GRID_W, rows, S5_WIDTH)

    x_lat, x_ctx = x, ctx
    for l in range(depth):
        last = l == depth - 1
        mod_lat, mod_ctx = mod_all[l, :bsz], mod_all[l, bsz:bsz + 1]
        w_packed = _pack_w_in(w_in[l])
        qkv_l, gate_l, lr_l, lx_l, su_l = _inproj(x_lat, mod_lat, True, row(norm1[l]), w_packed, TM_LAT, True)
        qkv_c, gate_c, lr_c, lx_c, su_c = _inproj(x_ctx, mod_ctx, False, row(norm1[l]), w_packed, tm_ctx, False)
        su_seq = su_l.reshape(bsz, seq, S5_WIDTH)

        mix_l, mix_c = [], []
        for dr in range(N_DIR):
            rev = dr == 1
            upw = jnp.zeros((LR_PAD, GLA_KEY), F32).at[dr * GLA_RANK:(dr + 1) * GLA_RANK].set(gla_up_w[l, dr]).astype(BF16)
            upb = row(gla_up_b[l, dr])
            s0 = jnp.zeros((bsz, GLA_HEADS, GLA_DV, GLA_DK), F32)
            g_c, s_c = _gla(qkv_c, lr_c, upw, upb, s0, rev, tl_ctx)
            g_l, _ = _gla(qkv_l, lr_l, upw, upb, s_c, rev, tl_lat)
            lru_args = (lru_conv_w[l, dr], row(lru_conv_b[l, dr]),
                        _block_diag(lru_wa[l, dr]).astype(BF16), row(lru_ba[l, dr]),
                        _block_diag(lru_wx[l, dr]).astype(BF16), row(lru_bx[l, dr]),
                        row(lru_lambda[l, dr]))
            r_c, h_c = _lru(lx_c, *lru_args, jnp.zeros((bsz, LRU_WIDTH), F32), rev, tc_ctx)
            r_l, _ = _lru(lx_l, *lru_args, h_c, rev, tc_lat)
            s5_args = _s5_params(s5_lam_re[l, dr], s5_lam_im[l, dr], s5_log_dt[l, dr], s5_b_re[l, dr],
                                 s5_b_im[l, dr], s5_c_re[l, dr], s5_c_im[l, dr], bsz)
            y_c, x_c = _s5(su_c, *s5_args, jnp.zeros((2, bsz, S5_LANES), F32), rev, S5_CHUNK)
            y_l, _ = _s5(su_seq, *s5_args, x_c, rev, S5_CHUNK)
            mix_l.append((g_l, r_l, cm_view(y_l)))
            mix_c.append((g_c, r_c, y_c))

        merge_w = (row(gla_norm[l]), row(s5_d[l]), s5_glu_w[l].astype(BF16), row(s5_glu_b[l]),
                   w_out[l].astype(BF16), row(norm2[l]), w_ff1[l].astype(BF16), w_ff2[l].astype(BF16),
                   row(final_norm))
        (gf, lf, sf), (gb, lb, sb) = mix_l
        x_lat = _merge_mlp(x_lat, mod_lat, True, gf, gb, gate_l, lf, lb, sf, sb, su_l, *merge_w,
                           TM_LAT, last, True)
        if not last:
            (gf, lf, sf), (gb, lb, sb) = mix_c
            x_ctx = _merge_mlp(x_ctx, mod_ctx, False, gf, gb, gate_c, lf, lb, sf, sb, su_c, *merge_w,
                               tm_ctx, False, False)
    return x_lat
```

```python
import functools

import jax
import jax.numpy as jnp
from jax import lax
from jax.experimental import pallas as pl
from jax.experimental.pallas import tpu as pltpu

F32 = jnp.float32
BF16 = jnp.bfloat16

LANES = 128
D_MODEL = 1024
GRID_W = 64
N_DIR = 2
EPS = 1e-6

GLA_HEADS = 4
GLA_VAL = 512
GLA_KEY = 256
GLA_DV = 128
GLA_DK = 64
GLA_RANK = 16
GLA_TAU = 16.0
GLA_CHUNK = 64
GLA_PAIR = 2 * GLA_CHUNK

LRU_WIDTH = 256
LRU_BLOCKS = 4
LRU_BLOCK = 64
LRU_CONV = 4
LRU_C = 8.0

S5_WIDTH = 256
S5_GROUP = 16
S5_GROUPS = 16
S5_STATE = 64
S5_LANES = S5_GROUPS * S5_STATE
S5_MXU_TILE = 256
S5_CHUNK = 128
S5_CHUNKS_PER_STEP = 2

D_FF = 4 * D_MODEL
FF_CHUNK = 1024

LR_PAD = 128
COL_QKV = (0, 1024)
COL_GATE = (1024, 1792)
COL_LR = (1792, 1920)
COL_LX = (1920, 2176)
COL_SU = (2176, 2432)
IN_PACKED = 2432
GRID_ROWS_PER_TILE = 8
TM_LAT = GRID_ROWS_PER_TILE * GRID_W

VMEM_LIMIT = 56 * 1024 * 1024


def _rmsnorm_rows(x, g):
    ms = jnp.mean(x * x, axis=-1, keepdims=True)
    return x * lax.rsqrt(ms + EPS) * g


def _gelu_tanh(x):
    c = 0.7978845608028654
    return 0.5 * x * (1.0 + jnp.tanh(c * (x + 0.044715 * (x * x * x))))


def _softplus(x):
    return jnp.maximum(x, 0.0) + jnp.log1p(jnp.exp(-jnp.abs(x)))


def _dot(a, b):
    return jnp.dot(a, b, preferred_element_type=F32)


def _mod_kernel(c_ref, w_ref, b_ref, o_ref):
    cs = c_ref[...]
    s = cs * jax.nn.sigmoid(cs)
    o_ref[0] = _dot(s.astype(BF16), w_ref[0].astype(BF16)) + b_ref[0]


def _modulation(cc, w_mod, b_mod):
    depth, d, n = w_mod.shape
    rows = cc.shape[0]
    tn = 1536
    return pl.pallas_call(
        _mod_kernel,
        grid=(depth, n // tn),
        in_specs=[
            pl.BlockSpec((rows, d), lambda l, j: (0, 0)),
            pl.BlockSpec((1, d, tn), lambda l, j: (l, 0, j)),
            pl.BlockSpec((1, 1, tn), lambda l, j: (l, 0, j)),
        ],
        out_specs=pl.BlockSpec((1, rows, tn), lambda l, j: (l, 0, j)),
        out_shape=jax.ShapeDtypeStruct((depth, rows, n), F32),
        compiler_params=pltpu.CompilerParams(
            dimension_semantics=("parallel", "parallel"), vmem_limit_bytes=VMEM_LIMIT),
        name="modulation",
    )(cc, w_mod, b_mod.reshape(depth, 1, n))


def _inproj_kernel(x_ref, m_ref, g_ref, w_ref, qkv_ref, gate_ref, lr_ref, lx_ref, su_ref, *, colmajor):
    h = _rmsnorm_rows(x_ref[0], g_ref[...]) * (1.0 + m_ref[0, 1:2, :]) + m_ref[0, 0:1, :]
    hb = h.astype(BF16)
    for ref, (c0, c1) in ((qkv_ref, COL_QKV), (gate_ref, COL_GATE), (lr_ref, COL_LR), (lx_ref, COL_LX)):
        ref[0] = _dot(hb, w_ref[:, c0:c1])
    su = _dot(hb, w_ref[:, COL_SU[0]:COL_SU[1]])
    if colmajor:
        su_ref[0] = jnp.swapaxes(su.reshape(GRID_ROWS_PER_TILE, GRID_W, S5_WIDTH), 0, 1)
    else:
        su_ref[0] = su


def _inproj(x, mod, mod_per_batch, norm_g, w_packed, tm, colmajor):
    b, l, d = x.shape
    widths = [c1 - c0 for c0, c1 in (COL_QKV, COL_GATE, COL_LR, COL_LX)]
    mod_map = (lambda i, j: (i, 0, 0)) if mod_per_batch else (lambda i, j: (0, 0, 0))
    tok = lambda w: pl.BlockSpec((1, tm, w), lambda i, j: (i, j, 0))
    if colmajor:
        assert tm == TM_LAT
        su_spec = pl.BlockSpec((1, GRID_W, GRID_ROWS_PER_TILE, S5_WIDTH), lambda i, j: (i, 0, j, 0))
        su_shape = jax.ShapeDtypeStruct((b, GRID_W, l // GRID_W, S5_WIDTH), F32)
    else:
        su_spec, su_shape = tok(S5_WIDTH), jax.ShapeDtypeStruct((b, l, S5_WIDTH), F32)
    return pl.pallas_call(
        functools.partial(_inproj_kernel, colmajor=colmajor),
        grid=(b, l // tm),
        in_specs=[
            tok(d),
            pl.BlockSpec((1, 6, d), mod_map),
            pl.BlockSpec((1, d), lambda i, j: (0, 0)),
            pl.BlockSpec((d, IN_PACKED), lambda i, j: (0, 0)),
        ],
        out_specs=[tok(w) for w in widths] + [su_spec],
        out_shape=[jax.ShapeDtypeStruct((b, l, w), F32) for w in widths] + [su_shape],
        compiler_params=pltpu.CompilerParams(
            dimension_semantics=("parallel", "parallel"), vmem_limit_bytes=VMEM_LIMIT),
        name="inproj",
    )(x, mod, norm_g, w_packed)


def _gla_kernel(qkv_ref, lr_ref, upw_ref, upb_ref, s0_ref, o_ref, sfin_ref, st_ref, *, rev, tl):
    j = pl.program_id(1)

    @pl.when(j == 0)
    def _():
        st_ref[...] = s0_ref[0]

    c = GLA_CHUNK
    pr = GLA_PAIR
    nchunk = tl // c
    npair = tl // pr
    row = lax.broadcasted_iota(jnp.int32, (pr, pr), 0)
    col = lax.broadcasted_iota(jnp.int32, (pr, pr), 1)
    keep = ((row // c) == (col // c)) & ((col >= row) if rev else (col <= row))
    tri = keep.astype(BF16)
    nt_dims = (((1,), (1,)), ((), ()))
    tn_dims = (((0,), (0,)), ((), ()))
    order = (lambda n: range(n - 1, -1, -1)) if rev else range
    crow = lambda ci: slice(ci * c, (ci + 1) * c)
    prow = lambda pi: slice(pi * pr, (pi + 1) * pr)
    kcol = lambda h: slice(h * GLA_DK, (h + 1) * GLA_DK)
    vcol = lambda h: slice(h * GLA_DV, (h + 1) * GLA_DV)
    heads = range(GLA_HEADS)

    logit = _dot(lr_ref[0].astype(BF16), upw_ref[...]) + upb_ref[...]
    log_a = (jnp.minimum(logit, 0.0) - jnp.log(1.0 + jnp.exp(-jnp.abs(logit)))) * (1.0 / GLA_TAU)
    hi = log_a.astype(BF16)
    lo = (log_a - hi.astype(F32)).astype(BF16)
    bcum = jnp.concatenate([_dot(tri, hi[prow(pi)]) + _dot(tri, lo[prow(pi)]) for pi in range(npair)], axis=0)
    blast = [bcum[ci * c:ci * c + 1, :] if rev else bcum[(ci + 1) * c - 1:(ci + 1) * c, :]
             for ci in range(nchunk)]
    btot = jnp.concatenate([jnp.broadcast_to(bl, (c, GLA_KEY)) for bl in blast], axis=0)
    q = qkv_ref[0, :, 0:GLA_KEY]
    k = qkv_ref[0, :, GLA_KEY:2 * GLA_KEY]
    v = qkv_ref[0, :, 2 * GLA_KEY:2 * GLA_KEY + GLA_VAL].astype(BF16)
    q_dec = ((q * (GLA_DK ** -0.5)) * jnp.exp(bcum)).astype(BF16)
    k_inv = (k * jnp.exp(-bcum)).astype(BF16)
    k_end = (k * jnp.exp(btot - bcum)).astype(BF16)
    decay = [jnp.exp(bl) for bl in blast]
    sc = {(pi, h): lax.dot_general(q_dec[prow(pi), kcol(h)], k_inv[prow(pi), kcol(h)], nt_dims,
                                   preferred_element_type=F32)
          for pi in range(npair) for h in heads}
    scm = {key: jnp.where(keep, val, 0.0).astype(BF16) for key, val in sc.items()}
    o_intra = {(pi, h): _dot(scm[pi, h], v[prow(pi), vcol(h)]) for pi in range(npair) for h in heads}
    ds = {(ci, h): lax.dot_general(v[crow(ci), vcol(h)], k_end[crow(ci), kcol(h)], tn_dims,
                                   preferred_element_type=F32)
          for ci in range(nchunk) for h in heads}
    s_in = {}
    for h in heads:
        st = st_ref[h]
        for ci in order(nchunk):
            s_in[ci, h] = st.astype(BF16)
            st = st * decay[ci][:, kcol(h)] + ds[ci, h]
        st_ref[h] = st
    o_inter = {(ci, h): lax.dot_general(q_dec[crow(ci), kcol(h)], s_in[ci, h], nt_dims,
                                        preferred_element_type=F32)
               for ci in range(nchunk) for h in heads}
    for ci in range(nchunk):
        half = slice((ci % 2) * c, (ci % 2 + 1) * c)
        o_ref[0, crow(ci), :] = jnp.concatenate(
            [o_intra[ci // 2, h][half] + o_inter[ci, h] for h in heads], axis=-1)

    @pl.when(j == pl.num_programs(1) - 1)
    def _():
        sfin_ref[0] = st_ref[...]


def _gla(qkv, lr, upw, upb, s0, rev, tl):
    b, l, _ = qkv.shape
    nt = l // tl
    tmap = (lambda i, j: (i, nt - 1 - j, 0)) if rev else (lambda i, j: (i, j, 0))
    state_shape = (b, GLA_HEADS, GLA_DV, GLA_DK)
    return pl.pallas_call(
        functools.partial(_gla_kernel, rev=rev, tl=tl),
        grid=(b, nt),
        in_specs=[
            pl.BlockSpec((1, tl, 2 * GLA_KEY + GLA_VAL), tmap),
            pl.BlockSpec((1, tl, LR_PAD), tmap),
            pl.BlockSpec((LR_PAD, GLA_KEY), lambda i, j: (0, 0)),
            pl.BlockSpec((1, GLA_KEY), lambda i, j: (0, 0)),
            pl.BlockSpec((1,) + state_shape[1:], lambda i, j: (i, 0, 0, 0)),
        ],
        out_specs=[
            pl.BlockSpec((1, tl, GLA_VAL), tmap),
            pl.BlockSpec((1,) + state_shape[1:], lambda i, j: (i, 0, 0, 0)),
        ],
        out_shape=[jax.ShapeDtypeStruct((b, l, GLA_VAL), F32), jax.ShapeDtypeStruct(state_shape, F32)],
        scratch_shapes=[pltpu.VMEM(state_shape[1:], F32)],
        compiler_params=pltpu.CompilerParams(
            dimension_semantics=("parallel", "arbitrary"), vmem_limit_bytes=VMEM_LIMIT),
        name="gla_rev" if rev else "gla_fwd",
    )(qkv, lr, upw, upb, s0)


def _to_time_major(src_ref, slab_ref, nb, nt):
    nslab = slab_ref.shape[0]
    for b in range(nb):
        for s in range(nslab):
            slab_ref[s, pl.ds(b, nt, stride=nb), :] = src_ref[b, :, s * LANES:(s + 1) * LANES]
    flat = jnp.concatenate([slab_ref[s] for s in range(nslab)], axis=-1)
    return flat.reshape(nt, nb, nslab * LANES)


def _slabs_to_block(slab_ref, dst_ref, nb, nt):
    for b in range(nb):
        for s in range(slab_ref.shape[0]):
            dst_ref[b, :, s * LANES:(s + 1) * LANES] = slab_ref[s, pl.ds(b, nt, stride=nb), :]


def _lru_kernel(x_ref, cw_ref, cb_ref, wa_ref, ba_ref, wx_ref, bx_ref, lam_ref, h0_ref,
                o_ref, hfin_ref, halo_ref, h_ref, a_scr, b_scr, slab_ref, *, rev, tc):
    j = pl.program_id(0)
    nb = x_ref.shape[0]
    taps = LRU_CONV - 1

    @pl.when(j == 0)
    def _():
        halo_ref[...] = jnp.zeros_like(halo_ref)
        h_ref[...] = h0_ref[...]

    xt = _to_time_major(x_ref, slab_ref, nb, tc)
    cw = cw_ref[...]
    if rev:
        ext = jnp.concatenate([xt, halo_ref[...]], axis=0)
        xc = sum(cw[k:k + 1, :] * ext[taps - k:taps - k + tc] for k in range(LRU_CONV))
        halo_ref[...] = xt[:taps]
    else:
        ext = jnp.concatenate([halo_ref[...], xt], axis=0)
        xc = sum(cw[k:k + 1, :] * ext[k:k + tc] for k in range(LRU_CONV))
        halo_ref[...] = xt[tc - taps:]
    xc = (xc + cb_ref[...]).reshape(tc * nb, LRU_WIDTH)
    xb = xc.astype(BF16)
    r = jax.nn.sigmoid(_dot(xb, wa_ref[...]) + ba_ref[...])
    i = jax.nn.sigmoid(_dot(xb, wx_ref[...]) + bx_ref[...])
    log_a = (-LRU_C) * r * _softplus(-lam_ref[...])
    a = jnp.exp(log_a)
    a_scr[...] = a.reshape(tc, nb, LRU_WIDTH)
    b_scr[...] = (jnp.sqrt(1.0 - a * a) * (i * xc)).reshape(tc, nb, LRU_WIDTH)

    def step(s, h):
        t = (tc - 1 - s) if rev else s
        h = a_scr[t] * h + b_scr[t]
        for sl in range(LRU_WIDTH // LANES):
            slab_ref[sl, pl.ds(pl.multiple_of(t * nb, nb), nb), :] = h[:, sl * LANES:(sl + 1) * LANES]
        return h

    h = lax.fori_loop(0, tc, step, h_ref[...], unroll=8)
    h_ref[...] = h
    _slabs_to_block(slab_ref, o_ref, nb, tc)

    @pl.when(j == pl.num_programs(0) - 1)
    def _():
        hfin_ref[...] = h


def _lru(lx, cw, cb, wa, ba, wx, bx, lam, h0, rev, tc):
    b, l, _ = lx.shape
    nt = l // tc
    tmap = (lambda j: (0, nt - 1 - j, 0)) if rev else (lambda j: (0, j, 0))
    const2 = lambda j: (0, 0)
    w = LRU_WIDTH
    return pl.pallas_call(
        functools.partial(_lru_kernel, rev=rev, tc=tc),
        grid=(nt,),
        in_specs=[
            pl.BlockSpec((b, tc, w), tmap),
            pl.BlockSpec((LRU_CONV, w), const2),
            pl.BlockSpec((1, w), const2),
            pl.BlockSpec((w, w), const2),
            pl.BlockSpec((1, w), const2),
            pl.BlockSpec((w, w), const2),
            pl.BlockSpec((1, w), const2),
            pl.BlockSpec((1, w), const2),
            pl.BlockSpec((b, w), const2),
        ],
        out_specs=[pl.BlockSpec((b, tc, w), tmap), pl.BlockSpec((b, w), const2)],
        out_shape=[jax.ShapeDtypeStruct((b, l, w), F32), jax.ShapeDtypeStruct((b, w), F32)],
        scratch_shapes=[
            pltpu.VMEM((LRU_CONV - 1, b, w), F32),
            pltpu.VMEM((b, w), F32),
            pltpu.VMEM((tc, b, w), F32),
            pltpu.VMEM((tc, b, w), F32),
            pltpu.VMEM((w // LANES, tc * b, LANES), F32),
        ],
        compiler_params=pltpu.CompilerParams(
            dimension_semantics=("arbitrary",), vmem_limit_bytes=VMEM_LIMIT),
        name="lru_rev" if rev else "lru_fwd",
    )(lx, cw, cb, wa, ba, wx, bx, lam, h0)


def _s5_kernel(u_ref, bbar_ref, cmat_ref, ar_ref, ai_ref, h0_ref, o_ref, hfin_ref,
               st_ref, buf_a, buf_b, u_slab, y_slab, *, rev, tc):
    j = pl.program_id(0)
    nb = u_ref.shape[0]
    n = S5_LANES
    tw = S5_MXU_TILE
    ntile = 2 * n // tw
    spt = tc // ntile

    @pl.when(j == 0)
    def _():
        st_ref[...] = h0_ref[...]

    nck = S5_CHUNKS_PER_STEP
    u = _to_time_major(u_ref, u_slab, nb, nck * tc).reshape(nck * tc * nb, S5_WIDTH).astype(BF16)
    order = list(range(nck - 1, -1, -1)) if rev else list(range(nck))
    chunk_rows = lambda ck: slice(ck * tc * nb, (ck + 1) * tc * nb)
    ar = ar_ref[...]
    ai = ai_ref[...]

    def proj_in(buf, uh, g):
        cols = slice(g * tw, (g + 1) * tw)
        buf[:, :, cols] = _dot(uh, bbar_ref[:, cols]).reshape(tc, nb, tw)

    def scan_steps(buf, carry, g):
        xr, xi = carry
        for s in range(g * spt, (g + 1) * spt):
            t = (tc - 1 - s) if rev else s
            nxr = ar * xr - ai * xi + buf[t, :, 0:n]
            nxi = ar * xi + ai * xr + buf[t, :, n:2 * n]
            buf[t, :, 0:n] = nxr
            buf[t, :, n:2 * n] = nxi
            xr, xi = nxr, nxi
        return xr, xi

    def proj_out(buf, acc, g):
        cols = slice(g * tw, (g + 1) * tw)
        return acc + _dot(buf[:, :, cols].reshape(tc * nb, tw).astype(BF16), cmat_ref[cols, :])

    bufs = (buf_a, buf_b)
    uk = [u[chunk_rows(ck)] for ck in order]
    ys = [jnp.zeros((tc * nb, S5_WIDTH), F32) for _ in order]
    for g in range(ntile):
        proj_in(bufs[0], uk[0], g)
    carry = (st_ref[0], st_ref[1])
    for k in range(nck):
        cur, other = bufs[k % 2], bufs[(k + 1) % 2]
        for g in range(ntile):
            carry = scan_steps(cur, carry, g)
            if k >= 1:
                ys[k - 1] = proj_out(other, ys[k - 1], g)
            if k + 1 < nck:
                proj_in(other, uk[k + 1], g)
    for g in range(ntile):
        ys[nck - 1] = proj_out(bufs[(nck - 1) % 2], ys[nck - 1], g)
    st_ref[0], st_ref[1] = carry

    for s in range(S5_WIDTH // LANES):
        for k, ck in enumerate(order):
            y_slab[s, chunk_rows(ck), :] = ys[k][:, s * LANES:(s + 1) * LANES]
    _slabs_to_block(y_slab, o_ref, nb, nck * tc)

    @pl.when(j == pl.num_programs(0) - 1)
    def _():
        hfin_ref[...] = st_ref[...]


def _s5(u_seq, bbar, cmat, ar, ai, h0, rev, tc):
    b, t, w = u_seq.shape
    n = S5_LANES
    ts = S5_CHUNKS_PER_STEP * tc
    nt = t // ts
    tmap = (lambda j: (0, nt - 1 - j, 0)) if rev else (lambda j: (0, j, 0))
    const2 = lambda j: (0, 0)
    const3 = lambda j: (0, 0, 0)
    return pl.pallas_call(
        functools.partial(_s5_kernel, rev=rev, tc=tc),
        grid=(nt,),
        in_specs=[
            pl.BlockSpec((b, ts, w), tmap),
            pl.BlockSpec((w, 2 * n), const2),
            pl.BlockSpec((2 * n, w), const2),
            pl.BlockSpec((b, n), const2),
            pl.BlockSpec((b, n), const2),
            pl.BlockSpec((2, b, n), const3),
        ],
        out_specs=[pl.BlockSpec((b, ts, w), tmap), pl.BlockSpec((2, b, n), const3)],
        out_shape=[jax.ShapeDtypeStruct((b, t, w), F32), jax.ShapeDtypeStruct((2, b, n), F32)],
        scratch_shapes=[
            pltpu.VMEM((2, b, n), F32),
            pltpu.VMEM((tc, b, 2 * n), F32),
            pltpu.VMEM((tc, b, 2 * n), F32),
            pltpu.VMEM((w // LANES, ts * b, LANES), F32),
            pltpu.VMEM((w // LANES, ts * b, LANES), F32),
        ],
        compiler_params=pltpu.CompilerParams(
            dimension_semantics=("arbitrary",), vmem_limit_bytes=VMEM_LIMIT),
        name="s5_rev" if rev else "s5_fwd",
    )(u_seq, bbar, cmat, ar, ai, h0)


def _merge_mlp_kernel(x_ref, m_ref, gf_ref, gb_ref, gate_ref, lf_ref, lb_ref, sf_ref, sb_ref, su_ref,
                      gn_ref, sd_ref, gluw_ref, glub_ref, wout_ref, n2_ref, w1_ref, w2_ref, fn_ref,
                      o_ref, *, final, colmajor):
    x = x_ref[0]
    go = gf_ref[0] + gb_ref[0]
    heads = []
    for h in range(GLA_HEADS):
        oh = go[:, h * GLA_DV:(h + 1) * GLA_DV]
        heads.append(oh * lax.rsqrt(jnp.mean(oh * oh, axis=-1, keepdims=True) + EPS))
    gg = gate_ref[0, :, 0:GLA_VAL]
    lg = gate_ref[0, :, GLA_VAL:GLA_VAL + LRU_WIDTH]
    o = jnp.concatenate(heads, axis=-1) * gn_ref[...] * (gg * jax.nn.sigmoid(gg))
    r = (lf_ref[0] + lb_ref[0]) * _gelu_tanh(lg)
    s = sf_ref[0] + sb_ref[0] + sd_ref[...] * su_ref[0]
    if colmajor:
        s = jnp.swapaxes(s, 0, 1).reshape(TM_LAT, S5_WIDTH)
    s = _gelu_tanh(s)
    s = s * jax.nn.sigmoid(_dot(s.astype(BF16), gluw_ref[...]) + glub_ref[...])
    cat = jnp.concatenate([o, r, s], axis=-1).astype(BF16)
    x1 = x + m_ref[0, 2:3, :] * _dot(cat, wout_ref[...])
    h2 = _rmsnorm_rows(x1, n2_ref[...]) * (1.0 + m_ref[0, 4:5, :]) + m_ref[0, 3:4, :]
    hb = h2.astype(BF16)
    acc = jnp.zeros_like(x1)
    for c in range(D_FF // FF_CHUNK):
        cs = slice(c * FF_CHUNK, (c + 1) * FF_CHUNK)
        t = jnp.maximum(_dot(hb, w1_ref[:, cs]), 0.0)
        acc = acc + _dot((t * t).astype(BF16), w2_ref[cs, :])
    x2 = x1 + m_ref[0, 5:6, :] * acc
    if final:
        x2 = _rmsnorm_rows(x2, fn_ref[...])
    o_ref[0] = x2


def _merge_mlp(x, mod, mod_per_batch, gla_f, gla_b, gate, lru_f, lru_b, s5_f, s5_b, su,
               gn, sd, gluw, glub, wout, n2, w1, w2, fn, tm, final, colmajor):
    b, l, d = x.shape
    tok = lambda w: pl.BlockSpec((1, tm, w), lambda i, j: (i, j, 0))
    if colmajor:
        assert tm == TM_LAT
        s5_tok = pl.BlockSpec((1, GRID_W, GRID_ROWS_PER_TILE, S5_WIDTH), lambda i, j: (i, 0, j, 0))
    else:
        s5_tok = tok(S5_WIDTH)
    mod_map = (lambda i, j: (i, 0, 0)) if mod_per_batch else (lambda i, j: (0, 0, 0))
    const = lambda shape: pl.BlockSpec(shape, lambda i, j: (0, 0), pipeline_mode=pl.Buffered(1))
    return pl.pallas_call(
        functools.partial(_merge_mlp_kernel, final=final, colmajor=colmajor),
        grid=(b, l // tm),
        in_specs=[
            tok(d),
            pl.BlockSpec((1, 6, d), mod_map),
            tok(GLA_VAL), tok(GLA_VAL), tok(GLA_VAL + LRU_WIDTH),
            tok(LRU_WIDTH), tok(LRU_WIDTH), s5_tok, s5_tok, s5_tok,
            const((1, GLA_VAL)), const((1, S5_WIDTH)), const((S5_WIDTH, S5_WIDTH)), const((1, S5_WIDTH)),
            const((d, d)), const((1, d)), const((d, D_FF)), const((D_FF, d)), const((1, d)),
        ],
        out_specs=tok(d),
        out_shape=jax.ShapeDtypeStruct((b, l, d), F32),
        compiler_params=pltpu.CompilerParams(
            dimension_semantics=("parallel", "parallel"), vmem_limit_bytes=VMEM_LIMIT),
        name="merge_mlp_final" if final else "merge_mlp",
    )(x, mod, gla_f, gla_b, gate, lru_f, lru_b, s5_f, s5_b, su, gn, sd, gluw, glub, wout, n2, w1, w2, fn)


def _block_diag(blocks):
    n, r, c = blocks.shape
    eye = jnp.eye(n, dtype=blocks.dtype)
    return (eye[:, None, :, None] * blocks[:, :, None, :]).reshape(n * r, n * c)


def _pack_w_in(w):
    d = w.shape[0]
    q_k_v_gg = w[:, 0:1536]
    lr = w[:, 1536:1568]
    lx = w[:, 1568:1824]
    lg = w[:, 1824:2080]
    su = w[:, 2080:2336]
    pad = jnp.zeros((d, LR_PAD - lr.shape[1]), w.dtype)
    return jnp.concatenate([q_k_v_gg, lg, lr, pad, lx, su], axis=1).astype(BF16)


def _s5_params(lam_re, lam_im, log_dt, b_re, b_im, c_re, c_im, nb):
    dt = jnp.exp(log_dt)[:, None]
    mag = jnp.exp(lam_re * dt)
    ang = lam_im * dt
    abar_r, abar_i = mag * jnp.cos(ang), mag * jnp.sin(ang)
    den = lam_re * lam_re + lam_im * lam_im
    num_r = abar_r - 1.0
    coef_r = (num_r * lam_re + abar_i * lam_im) / den
    coef_i = (abar_i * lam_re - num_r * lam_im) / den
    bbar_r = coef_r[..., None] * b_re - coef_i[..., None] * b_im
    bbar_i = coef_r[..., None] * b_im + coef_i[..., None] * b_re
    bd_in = lambda m: _block_diag(jnp.swapaxes(m, 1, 2))
    bbar = jnp.concatenate([bd_in(bbar_r), bd_in(bbar_i)], axis=1).astype(BF16)
    bd_out = lambda m: _block_diag(jnp.swapaxes(m, 1, 2))
    cmat = jnp.concatenate([bd_out(c_re), -bd_out(c_im)], axis=0).astype(BF16)
    ar = jnp.broadcast_to(abar_r.reshape(1, S5_LANES), (nb, S5_LANES))
    ai = jnp.broadcast_to(abar_i.reshape(1, S5_LANES), (nb, S5_LANES))
    return bbar, cmat, ar, ai


def _tile(n, pref):
    t = min(n, pref)
    while n % t:
        t //= 2
    return t


def kernel(x, c, ctx, c_ctx, w_mod, b_mod, norm1, norm2, w_in, gla_up_w, gla_up_b, gla_norm, lru_conv_w, lru_conv_b, lru_wa, lru_ba, lru_wx, lru_bx, lru_lambda, s5_lam_re, s5_lam_im, s5_log_dt, s5_b_re, s5_b_im, s5_c_re, s5_c_im, s5_d, s5_glu_w, s5_glu_b, w_out, w_ff1, w_ff2, final_norm):
    bsz, seq, d = x.shape
    ctx_len = ctx.shape[1]
    depth = w_mod.shape[0]
    rows = seq // GRID_W
    assert d == D_MODEL and seq == rows * GRID_W and rows % GRID_ROWS_PER_TILE == 0
    s5_steps = S5_CHUNKS_PER_STEP * S5_CHUNK
    assert seq % s5_steps == 0 and ctx_len % s5_steps == 0 and ctx_len % GLA_PAIR == 0

    cc = jnp.concatenate([c, c_ctx[None, :], jnp.zeros((7, d), F32)], axis=0)
    mod_all = _modulation(cc, w_mod, b_mod).reshape(depth, bsz + 8, 6, d)

    tm_ctx = _tile(ctx_len, 512)
    tl_lat, tl_ctx = _tile(seq, 2048), _tile(ctx_len, 2048)
    tc_lat, tc_ctx = _tile(seq, 512), _tile(ctx_len, 512)
    row = lambda v: v.reshape(1, -1)
    cm_view = lambda t: t.reshape(bsz, GRID_W, rows, S5_WIDTH)

    x_lat, x_ctx = x, ctx
    for l in range(depth):
        last = l == depth - 1
        mod_lat, mod_ctx = mod_all[l, :bsz], mod_all[l, bsz:bsz + 1]
        w_packed = _pack_w_in(w_in[l])
        qkv_l, gate_l, lr_l, lx_l, su_l = _inproj(x_lat, mod_lat, True, row(norm1[l]), w_packed, TM_LAT, True)
        qkv_c, gate_c, lr_c, lx_c, su_c = _inproj(x_ctx, mod_ctx, False, row(norm1[l]), w_packed, tm_ctx, False)
        su_seq = su_l.reshape(bsz, seq, S5_WIDTH)

        mix_l, mix_c = [], []
        for dr in range(N_DIR):
            rev = dr == 1
            upw = jnp.zeros((LR_PAD, GLA_KEY), F32).at[dr * GLA_RANK:(dr + 1) * GLA_RANK].set(gla_up_w[l, dr]).astype(BF16)
            upb = row(gla_up_b[l, dr])
            s0 = jnp.zeros((bsz, GLA_HEADS, GLA_DV, GLA_DK), F32)
            g_c, s_c = _gla(qkv_c, lr_c, upw, upb, s0, rev, tl_ctx)
            g_l, _ = _gla(qkv_l, lr_l, upw, upb, s_c, rev, tl_lat)
            lru_args = (lru_conv_w[l, dr], row(lru_conv_b[l, dr]),
                        _block_diag(lru_wa[l, dr]).astype(BF16), row(lru_ba[l, dr]),
                        _block_diag(lru_wx[l, dr]).astype(BF16), row(lru_bx[l, dr]),
                        row(lru_lambda[l, dr]))
            r_c, h_c = _lru(lx_c, *lru_args, jnp.zeros((bsz, LRU_WIDTH), F32), rev, tc_ctx)
            r_l, _ = _lru(lx_l, *lru_args, h_c, rev, tc_lat)
            s5_args = _s5_params(s5_lam_re[l, dr], s5_lam_im[l, dr], s5_log_dt[l, dr], s5_b_re[l, dr],
                                 s5_b_im[l, dr], s5_c_re[l, dr], s5_c_im[l, dr], bsz)
            y_c, x_c = _s5(su_c, *s5_args, jnp.zeros((2, bsz, S5_LANES), F32), rev, S5_CHUNK)
            y_l, _ = _s5(su_seq, *s5_args, x_c, rev, S5_CHUNK)
            mix_l.append((g_l, r_l, cm_view(y_l)))
            mix_c.append((g_c, r_c, y_c))

        merge_w = (row(gla_norm[l]), row(s5_d[l]), s5_glu_w[l].astype(BF16), row(s5_glu_b[l]),
                   w_out[l].astype(BF16), row(norm2[l]), w_ff1[l].astype(BF16), w_ff2[l].astype(BF16),
                   row(final_norm))
        (gf, lf, sf), (gb, lb, sb) = mix_l
        x_lat = _merge_mlp(x_lat, mod_lat, True, gf, gb, gate_l, lf, lb, sf, sb, su_l, *merge_w,
                           TM_LAT, last, True)
        if not last:
            (gf, lf, sf), (gb, lb, sb) = mix_c
            x_ctx = _merge_mlp(x_ctx, mod_ctx, False, gf, gb, gate_c, lf, lb, sf, sb, su_c, *merge_w,
                               tm_ctx, False, False)
    return x_lat
```
